```python
import math
import numpy as np
import jax
import jax.numpy as jnp
from jax import lax


D_MODEL = 2048
BATCH = 1
SEQ = 8192
DEPTH = 2

GRID_W = 64
CTX_LEN = 256
EPS = 1e-6
ROPE_BASE = 10000.0

SSD_EXPAND = 2
D_INNER = SSD_EXPAND * D_MODEL
SSD_HEADDIM = 64
SSD_HEADS = D_INNER // SSD_HEADDIM
SSD_GROUPS = 8
SSD_HPG = SSD_HEADS // SSD_GROUPS
SSD_STATE = 128
SSD_CONV = 5
SSD_CHUNK = 128
CONV_DIM = D_INNER + 2 * SSD_GROUPS * SSD_STATE

NA_HEAD_DIM = 128
NA_HEADS = D_MODEL // NA_HEAD_DIM
NA_WIDTH = NA_HEADS * NA_HEAD_DIM
NA_WIN_H = 8
NA_WIN_W = 16
NA_QCB = NA_WIN_W
NA_KCB = 2 * NA_WIN_W
NA_NCB = GRID_W // NA_QCB

D_FF = 4 * D_MODEL

IN_SIZES = (D_INNER, CONV_DIM, 2 * SSD_HEADS, NA_WIDTH, NA_WIDTH, NA_WIDTH, D_MODEL, D_MODEL)
IN_DIM = D_INNER + CONV_DIM + 2 * SSD_HEADS + 3 * NA_WIDTH + 2 * D_MODEL

kernel_name = 'hybrid_ssd_natten_dit_block'


def rmsnorm(x, g):
    x32 = x.astype(jnp.float32)
    y = x32 * lax.rsqrt(jnp.mean(x32 * x32, axis=-1, keepdims=True) + EPS)
    return y.astype(x.dtype) * g


def modulate(x, shift, scale):
    return x * (1 + scale) + shift


def split_cols(p):
    return jnp.split(p, np.cumsum(IN_SIZES)[:-1].tolist(), axis=-1)


def axial_rope(x, row, col):
    n_ax = x.shape[-1] // 2
    inv = ROPE_BASE ** (-jnp.arange(0, n_ax, 2, dtype=jnp.float32) / n_ax)

    def rot(xa, pos):
        ang = pos.astype(jnp.float32)[:, None] * inv
        cos = jnp.cos(ang)[:, None, :].astype(x.dtype)
        sin = jnp.sin(ang)[:, None, :].astype(x.dtype)
        x1, x2 = jnp.split(xa, 2, axis=-1)
        return jnp.concatenate([x1 * cos - x2 * sin, x2 * cos + x1 * sin], axis=-1)

    xr, xc = jnp.split(x, 2, axis=-1)
    return jnp.concatenate([rot(xr, row), rot(xc, col)], axis=-1)


def dwconv_centred(x, w, b):
    k = w.shape[0]
    y = lax.conv_general_dilated(x, w[:, None, :].astype(x.dtype), window_strides=(1,),
                                 padding=((k // 2, k // 2),),
                                 dimension_numbers=('NWC', 'WIO', 'NWC'),
                                 feature_group_count=x.shape[-1])
    return y + b


def gated_rmsnorm(y, z, w):
    g = (y * jax.nn.silu(z)).astype(jnp.float32)
    gs = g.reshape(g.shape[:-1] + (SSD_GROUPS, -1))
    gs = gs * lax.rsqrt(jnp.mean(gs * gs, axis=-1, keepdims=True) + EPS)
    return gs.reshape(g.shape).astype(y.dtype) * w


def segsum(a):
    L = a.shape[-1]
    xa = jnp.broadcast_to(a[..., :, None], a.shape + (L,))
    xa = jnp.where(jnp.tril(jnp.ones((L, L), dtype=bool), -1), xa, 0.0)
    cs = jnp.cumsum(xa, axis=-2)
    return jnp.where(jnp.tril(jnp.ones((L, L), dtype=bool), 0), cs, -jnp.inf)


def ssd_scan(xh, dt, a, bm, cm, h0):
    b_, T = xh.shape[0], xh.shape[1]
    nc, L = T // SSD_CHUNK, SSD_CHUNK
    dtype = xh.dtype
    da = (dt * a).reshape(b_, nc, L, SSD_GROUPS, SSD_HPG)
    da = jnp.transpose(da, (0, 1, 3, 4, 2))
    da_cs = jnp.cumsum(da, axis=-1)
    xdt = (xh * dt[..., None].astype(dtype)).reshape(b_, nc, L, SSD_GROUPS, SSD_HPG, SSD_HEADDIM)
    bc = bm.reshape(b_, nc, L, SSD_GROUPS, SSD_STATE)
    cc = cm.reshape(b_, nc, L, SSD_GROUPS, SSD_STATE)
    decay_in = jnp.exp(segsum(da)).astype(dtype)
    cb = jnp.einsum('bclgn,bcsgn->bcgls', cc, bc)
    y_diag = jnp.einsum('bcgls,bcgels,bcsgep->bclgep', cb, decay_in, xdt)
    decay_to_end = jnp.exp(da_cs[..., -1:] - da_cs).astype(dtype)
    chunk_states = jnp.einsum('bclgn,bcgel,bclgep->bcgepn', bc, decay_to_end, xdt)
    chunk_decay = jnp.exp(da_cs[..., -1]).astype(dtype)

    def step(h, inp):
        dec, st = inp
        return dec[..., None, None] * h + st, h

    h_final, h_in = lax.scan(step, h0.astype(dtype),
                             (jnp.moveaxis(chunk_decay, 1, 0), jnp.moveaxis(chunk_states, 1, 0)))
    h_in = jnp.moveaxis(h_in, 0, 1)
    y_off = jnp.einsum('bclgn,bcgepn,bcgel->bclgep', cc, h_in, jnp.exp(da_cs).astype(dtype))
    y = (y_diag + y_off).reshape(b_, T, SSD_HEADS, SSD_HEADDIM)
    return y, h_final


def ssd_branch(z_l, xbc_l, dt_l, z_c, xbc_c, dt_c, conv_w, conv_b, a_log, dt_bias, d_skip,
               norm_w, w_o, row, col, with_ctx_out):
    a = -jnp.exp(a_log.astype(jnp.float32))

    def prep(xbc, dt_raw, use_rope):
        b_, T = xbc.shape[0], xbc.shape[1]
        xbc = jax.nn.silu(dwconv_centred(xbc, conv_w, conv_b))
        xs, bm, cm = jnp.split(xbc, [D_INNER, D_INNER + SSD_GROUPS * SSD_STATE], axis=-1)
        xs = xs.reshape(b_, T, SSD_HEADS, SSD_HEADDIM)
        bm = bm.reshape(b_, T, SSD_GROUPS, SSD_STATE)
        cm = cm.reshape(b_, T, SSD_GROUPS, SSD_STATE)
        if use_rope:
            bm, cm = axial_rope(bm, row, col), axial_rope(cm, row, col)
        dt = jax.nn.softplus(dt_raw.astype(jnp.float32).reshape(b_, T, 2, SSD_HEADS) + dt_bias)
        return xs, bm, cm, dt

    xc_, bc_, cc_, dtc = prep(xbc_c, dt_c, False)
    xl_, bl_, cl_, dtl = prep(xbc_l, dt_l, True)
    b_ = xl_.shape[0]
    h0 = jnp.zeros((b_, SSD_GROUPS, SSD_HPG, SSD_HEADDIM, SSD_STATE), xl_.dtype)
    flip = lambda t: jnp.flip(t, axis=1)
    y_cf, h_cf = ssd_scan(xc_, dtc[:, :, 0], a[0], bc_, cc_, h0)
    y_cb, h_cb = ssd_scan(flip(xc_), flip(dtc[:, :, 1]), a[1], flip(bc_), flip(cc_), h0)
    y_lf, _ = ssd_scan(xl_, dtl[:, :, 0], a[0], bl_, cl_, h_cf)
    y_lb, _ = ssd_scan(flip(xl_), flip(dtl[:, :, 1]), a[1], flip(bl_), flip(cl_), h_cb)
    d_sum = (d_skip[0] + d_skip[1])[:, None].astype(xl_.dtype)

    def finish(y_f, y_b, xs, z):
        y = y_f + y_b + d_sum * xs
        y = gated_rmsnorm(y.reshape(xs.shape[0], xs.shape[1], D_INNER), z, norm_w)
        return y @ w_o

    y_lat = finish(y_lf, flip(y_lb), xl_, z_l)
    y_ctx = finish(y_cf, flip(y_cb), xc_, z_c) if with_ctx_out else None
    return y_lat, y_ctx


def neighbourhood_attention(q, k, v, k_ctx, v_ctx, rpb):
    b_, T, H, d = q.shape
    rows = T // GRID_W
    wh = min(NA_WIN_H, rows)
    qg = q.reshape(b_, rows, GRID_W, H, d)
    kg = k.reshape(b_, rows, GRID_W, H, d)
    vg = v.reshape(b_, rows, GRID_W, H, d)
    qcol = np.arange(GRID_W).reshape(NA_NCB, NA_QCB)
    kstart = np.clip(np.arange(NA_NCB) * NA_QCB - NA_WIN_W // 2, 0, GRID_W - NA_KCB)
    kcol = kstart[:, None] + np.arange(NA_KCB)[None, :]
    cstart = np.clip(qcol - NA_WIN_W // 2, 0, GRID_W - NA_WIN_W)
    col_ok = (kcol[:, None, :] >= cstart[..., None]) & (kcol[:, None, :] < cstart[..., None] + NA_WIN_W)
    dcol_idx = np.clip(kcol[:, None, :] - qcol[:, :, None], -(NA_WIN_W - 1), NA_WIN_W - 1) + NA_WIN_W - 1
    mask = np.broadcast_to(col_ok[:, :, None, :], (NA_NCB, NA_QCB, wh, NA_KCB)).reshape(NA_NCB, NA_QCB, wh * NA_KCB)
    nk = wh * NA_KCB
    scale = d ** -0.5

    def row_block(r):
        rs = jnp.clip(r - NA_WIN_H // 2, 0, rows - wh)
        qr = lax.dynamic_index_in_dim(qg, r, axis=1, keepdims=False).reshape(b_, NA_NCB, NA_QCB, H, d)
        kb = lax.dynamic_slice_in_dim(kg, rs, wh, axis=1)[:, :, kcol]
        vb = lax.dynamic_slice_in_dim(vg, rs, wh, axis=1)[:, :, kcol]
        kb = jnp.transpose(kb, (0, 2, 1, 3, 4, 5)).reshape(b_, NA_NCB, nk, H, d)
        vb = jnp.transpose(vb, (0, 2, 1, 3, 4, 5)).reshape(b_, NA_NCB, nk, H, d)
        drow_idx = rs + jnp.arange(wh) - r + NA_WIN_H - 1
        bias = rpb[:, drow_idx][:, :, dcol_idx]
        bias = jnp.transpose(bias, (0, 2, 3, 1, 4)).reshape(H, NA_NCB, NA_QCB, nk)
        s_loc = jnp.einsum('bjqhd,bjkhd->bhjqk', qr, kb).astype(jnp.float32) * scale + bias.astype(jnp.float32)
        s_loc = jnp.where(mask, s_loc, -jnp.inf)
        s_ctx = jnp.einsum('bjqhd,bkhd->bhjqk', qr, k_ctx).astype(jnp.float32) * scale
        p = jax.nn.softmax(jnp.concatenate([s_loc, s_ctx], axis=-1), axis=-1).astype(v.dtype)
        o = (jnp.einsum('bhjqk,bjkhd->bjqhd', p[..., :nk], vb)
             + jnp.einsum('bhjqk,bkhd->bjqhd', p[..., nk:], v_ctx))
        return o.reshape(b_, GRID_W, H, d)

    out = lax.map(row_block, jnp.arange(rows))
    return jnp.moveaxis(out, 0, 1).reshape(b_, T, H, d)


def context_attention(q, k, v):
    s = jnp.einsum('bqhd,bkhd->bhqk', q, k).astype(jnp.float32) * (q.shape[-1] ** -0.5)
    p = jax.nn.softmax(s, axis=-1).astype(v.dtype)
    return jnp.einsum('bhqk,bkhd->bqhd', p, v)


def mixer_block(u_l, u_c, w_in, conv_w, conv_b, a_log, dt_bias, d_skip, ssd_norm, w_ssd_o,
                rpb, w_na_o, w_out, row, col, with_ctx_out):
    z_l, xbc_l, dt_l, q_l, k_l, v_l, ga_l, gb_l = split_cols(u_l @ w_in)
    z_c, xbc_c, dt_c, q_c, k_c, v_c, ga_c, gb_c = split_cols(u_c @ w_in)
    y_ssd_l, y_ssd_c = ssd_branch(z_l, xbc_l, dt_l, z_c, xbc_c, dt_c, conv_w, conv_b, a_log, dt_bias,
                                  d_skip, ssd_norm, w_ssd_o, row, col, with_ctx_out)
    heads = lambda t: t.reshape(t.shape[0], t.shape[1], NA_HEADS, NA_HEAD_DIM)
    kc, vc = heads(k_c), heads(v_c)
    o_l = neighbourhood_attention(heads(q_l), heads(k_l), heads(v_l), kc, vc, rpb)
    y_na_l = o_l.reshape(u_l.shape[0], u_l.shape[1], NA_WIDTH) @ w_na_o
    out_l = (jax.nn.sigmoid(ga_l) * y_ssd_l + jax.nn.sigmoid(gb_l) * y_na_l) @ w_out
    if not with_ctx_out:
        return out_l, None
    o_c = context_attention(heads(q_c), kc, vc)
    y_na_c = o_c.reshape(u_c.shape[0], u_c.shape[1], NA_WIDTH) @ w_na_o
    out_c = (jax.nn.sigmoid(ga_c) * y_ssd_c + jax.nn.sigmoid(gb_c) * y_na_c) @ w_out
    return out_l, out_c


def sq_relu_mlp(h, w1, w2):
    return jnp.square(jax.nn.relu(h @ w1)) @ w2


def setup_inputs(seed: int = 0) -> dict:
    key = jax.random.key(seed)
    ks = jax.random.split(key, 24)
    f32 = jnp.float32
    L = DEPTH

    def nrm(k, shape, s):
        return jax.random.normal(k, shape, f32) * s

    def gain(k, shape):
        return 1.0 + 0.05 * jax.random.normal(k, shape, f32)

    dt0 = jnp.exp(jax.random.uniform(ks[14], (L, 2, SSD_HEADS), f32, math.log(1e-3), math.log(1e-1)))
    dt_bias = dt0 + jnp.log(-jnp.expm1(-dt0))
    return {
        'x': nrm(ks[0], (BATCH, SEQ, D_MODEL), 1.0),
        'c': nrm(ks[1], (BATCH, D_MODEL), 1.0),
        'ctx': nrm(ks[2], (BATCH, CTX_LEN, D_MODEL), 1.0),
        'c_ctx': nrm(ks[3], (D_MODEL,), 1.0),
        'w_ada': nrm(ks[4], (L, D_MODEL, 6 * D_MODEL), D_MODEL ** -0.5),
        'b_ada': nrm(ks[5], (L, 6 * D_MODEL), 0.02),
        'g_pre_mix': gain(ks[6], (L, D_MODEL)),
        'g_post_mix': gain(ks[7], (L, D_MODEL)),
        'g_pre_mlp': gain(ks[8], (L, D_MODEL)),
        'g_post_mlp': gain(ks[9], (L, D_MODEL)),
        'w_in': nrm(ks[10], (L, D_MODEL, IN_DIM), D_MODEL ** -0.5),
        'conv_w': nrm(ks[11], (L, SSD_CONV, CONV_DIM), SSD_CONV ** -0.5),
        'conv_b': nrm(ks[12], (L, CONV_DIM), 0.02),
        'a_log': jnp.log(jax.random.uniform(ks[13], (L, 2, SSD_HEADS), f32, 1.0, 16.0)),
        'dt_bias': dt_bias,
        'd_skip': gain(ks[15], (L, 2, SSD_HEADS)),
        'ssd_norm': gain(ks[16], (L, D_INNER)),
        'w_ssd_o': nrm(ks[17], (L, D_INNER, D_MODEL), D_INNER ** -0.5),
        'rpb': nrm(ks[18], (L, NA_HEADS, 2 * NA_WIN_H - 1, 2 * NA_WIN_W - 1), 0.1),
        'w_na_o': nrm(ks[19], (L, NA_WIDTH, D_MODEL), NA_WIDTH ** -0.5),
        'w_out': nrm(ks[20], (L, D_MODEL, D_MODEL), D_MODEL ** -0.5),
        'w_mlp1': nrm(ks[21], (L, D_MODEL, D_FF), D_MODEL ** -0.5),
        'w_mlp2': nrm(ks[22], (L, D_FF, D_MODEL), D_FF ** -0.5),
    }


def reference(x, c, ctx, c_ctx, w_ada, b_ada, g_pre_mix, g_post_mix, g_pre_mlp, g_post_mlp, w_in,
              conv_w, conv_b, a_log, dt_bias, d_skip, ssd_norm, w_ssd_o, rpb, w_na_o, w_out,
              w_mlp1, w_mlp2):
    T = x.shape[1]
    t = jnp.arange(T, dtype=jnp.int32)
    row, col = t // GRID_W, t % GRID_W
    for l in range(DEPTH):
        last = l == DEPTH - 1
        mod = jax.nn.silu(c) @ w_ada[l] + b_ada[l]
        mod_c = jax.nn.silu(c_ctx) @ w_ada[l] + b_ada[l]
        sh1, sc1, gt1, sh2, sc2, gt2 = jnp.split(mod[:, None, :], 6, axis=-1)
        csh1, csc1, cgt1, csh2, csc2, cgt2 = jnp.split(mod_c, 6, axis=-1)
        u_l = modulate(rmsnorm(x, g_pre_mix[l]), sh1, sc1)
        u_c = modulate(rmsnorm(ctx, g_pre_mix[l]), csh1, csc1)
        y_l, y_c = mixer_block(u_l, u_c, w_in[l], conv_w[l], conv_b[l], a_log[l], dt_bias[l], d_skip[l],
                               ssd_norm[l], w_ssd_o[l], rpb[l], w_na_o[l], w_out[l], row, col, not last)
        x = x + gt1 * rmsnorm(y_l, g_post_mix[l])
        h = modulate(rmsnorm(x, g_pre_mlp[l]), sh2, sc2)
        x = x + gt2 * rmsnorm(sq_relu_mlp(h, w_mlp1[l], w_mlp2[l]), g_post_mlp[l])
        if not last:
            ctx = ctx + cgt1 * rmsnorm(y_c, g_post_mix[l])
            hc = modulate(rmsnorm(ctx, g_pre_mlp[l]), csh2, csc2)
            ctx = ctx + cgt2 * rmsnorm(sq_relu_mlp(hc, w_mlp1[l], w_mlp2[l]), g_post_mlp[l])
    return x
```

```python
import functools

import numpy as np
import jax
import jax.numpy as jnp
from jax import lax
from jax.experimental import pallas as pl
from jax.experimental.pallas import tpu as pltpu

F32 = jnp.float32
BF16 = jnp.bfloat16

EPS = 1e-6
ROPE_BASE = 10000.0
GRID_W = 64
D_MODEL = 2048
D_INNER = 2 * D_MODEL
SSD_HEADDIM = 64
SSD_HEADS = D_INNER // SSD_HEADDIM
SSD_GROUPS = 8
SSD_HPG = SSD_HEADS // SSD_GROUPS
SSD_STATE = 128
SSD_CONV = 5
SSD_CHUNK = 128
GROUP_W = SSD_HPG * SSD_HEADDIM
BC_DIM = 2 * SSD_GROUPS * SSD_STATE
CONV_DIM = D_INNER + BC_DIM
NA_HEAD_DIM = 128
NA_HEADS = D_MODEL // NA_HEAD_DIM
NA_WIN_H = 8
NA_WIN_W = 16
D_FF = 4 * D_MODEL
OFF_Z = 0
OFF_XBC = D_INNER
OFF_DT = OFF_XBC + CONV_DIM
OFF_QKVG = OFF_DT + 2 * SSD_HEADS
QKVG_DIM = 5 * D_MODEL

LANE = 128
SUBLANE = 8
BF16_ROWS = 16
VMEM_LIMIT_MB = 56


def _cparams(n_axes, vmem_mb=VMEM_LIMIT_MB):
    return pltpu.CompilerParams(dimension_semantics=("arbitrary",) * n_axes,
                                vmem_limit_bytes=vmem_mb * 1024 * 1024)


def _pick_tile(rows, cap):
    best = None
    for d in range(BF16_ROWS, cap + 1, BF16_ROWS):
        if rows % d == 0:
            best = d
    assert best is not None, (rows, cap)
    return best


def _row_chunk(tm):
    return _pick_tile(tm, 256)


def _sigmoid(v):
    return 1.0 / (1.0 + jnp.exp(-v))


def _dot(a, b):
    return jnp.dot(a, b, preferred_element_type=F32)


def _dot_nt(a, b):
    return lax.dot_general(a, b, (((1,), (1,)), ((), ())), preferred_element_type=F32)


def _select_rows(ref, row0, rc, n_lat):
    rows = row0 + lax.broadcasted_iota(jnp.int32, (rc, 1), 0)
    return jnp.where(rows >= n_lat, ref[1:2, :], ref[0:1, :])


def _ada_kernel(c_ref, w_ref, b_ref, o_ref):
    cv = c_ref[...]
    s = cv * _sigmoid(cv)
    o_ref[...] = _dot(s.astype(BF16), w_ref[...].astype(BF16)) + b_ref[...]


def _ada_call(cc, w_ada, b_ada):
    depth, d, n6 = w_ada.shape
    tn = 1024
    return pl.pallas_call(
        _ada_kernel,
        grid=(depth, n6 // tn),
        in_specs=[pl.BlockSpec((SUBLANE, d), lambda l, j: (0, 0)),
                  pl.BlockSpec((None, d, tn), lambda l, j: (l, 0, j)),
                  pl.BlockSpec((None, 1, tn), lambda l, j: (l, 0, j))],
        out_specs=pl.BlockSpec((None, SUBLANE, tn), lambda l, j: (l, 0, j)),
        out_shape=jax.ShapeDtypeStruct((depth, SUBLANE, n6), F32),
        compiler_params=_cparams(2),
        name="ada",
    )(cc, w_ada, b_ada.reshape(depth, 1, n6))


def _norm_mm_kernel(x_ref, g_ref, sh_ref, sc_ref, w_ref, o_ref, u_ref, *, tm, n_lat, relu2):
    i = pl.program_id(0)
    rc = _row_chunk(tm)

    @pl.when(pl.program_id(1) == 0)
    def _():
        def body(k, carry):
            r0 = pl.multiple_of(k * rc, rc)
            xv = x_ref[pl.ds(r0, rc), :]
            ms = jnp.mean(xv * xv, axis=-1, keepdims=True)
            y = xv * lax.rsqrt(ms + EPS) * g_ref[...]
            sc = _select_rows(sc_ref, i * tm + r0, rc, n_lat)
            sh = _select_rows(sh_ref, i * tm + r0, rc, n_lat)
            u_ref[pl.ds(r0, rc), :] = (y * (1.0 + sc) + sh).astype(BF16)
            return carry
        lax.fori_loop(0, tm // rc, body, 0)

    acc = _dot(u_ref[...], w_ref[...].astype(BF16))
    if relu2:
        acc = jnp.square(jnp.maximum(acc, 0.0))
    o_ref[...] = acc.astype(o_ref.dtype)


def _norm_mm_call(x, g, mod, layer, sh_idx, sc_idx, w, col0, ncols, tn, out_dtype, *, rows, n_lat,
                  relu2=False, name):
    d = x.shape[1]
    tm = _pick_tile(rows, 1056)
    assert col0 % tn == 0 and ncols % tn == 0
    cb = col0 // tn
    kern = functools.partial(_norm_mm_kernel, tm=tm, n_lat=n_lat, relu2=relu2)
    return pl.pallas_call(
        kern,
        grid=(rows // tm, ncols // tn),
        in_specs=[pl.BlockSpec((tm, d), lambda i, j: (i, 0)),
                  pl.BlockSpec((1, d), lambda i, j: (0, 0)),
                  pl.BlockSpec((None, SUBLANE, d), lambda i, j: (layer, 0, sh_idx)),
                  pl.BlockSpec((None, SUBLANE, d), lambda i, j: (layer, 0, sc_idx)),
                  pl.BlockSpec((None, d, tn), lambda i, j: (layer, 0, cb + j))],
        out_specs=pl.BlockSpec((tm, tn), lambda i, j: (i, j)),
        out_shape=jax.ShapeDtypeStruct((rows, ncols), out_dtype),
        scratch_shapes=[pltpu.VMEM((tm, d), BF16)],
        compiler_params=_cparams(2),
        name=name,
    )(x, g, mod, mod, w)


def _mm_res_kernel(a_ref, w_ref, x_ref, g_ref, gt_ref, o_ref, acc_ref, *, tm, nk, n_lat):
    i = pl.program_id(0)
    k = pl.program_id(1)
    prod = _dot(a_ref[...], w_ref[...])

    @pl.when(k == 0)
    def _():
        acc_ref[...] = prod

    @pl.when(k > 0)
    def _():
        acc_ref[...] += prod

    rc = _row_chunk(tm)

    @pl.when(k == nk - 1)
    def _():
        def body(q, carry):
            r0 = pl.multiple_of(q * rc, rc)
            y = acc_ref[pl.ds(r0, rc), :]
            ms = jnp.mean(y * y, axis=-1, keepdims=True)
            r = y * lax.rsqrt(ms + EPS) * g_ref[...]
            gt = _select_rows(gt_ref, i * tm + r0, rc, n_lat)
            o_ref[pl.ds(r0, rc), :] = x_ref[pl.ds(r0, rc), :] + gt * r
            return carry
        lax.fori_loop(0, tm // rc, body, 0)


def _mm_res_call(a, w_bf16, x, g, mod, layer, gt_idx, tk, *, rows, n_lat, name):
    kdim, d = w_bf16.shape
    tm = _pick_tile(rows, 528)
    nk = kdim // tk
    kern = functools.partial(_mm_res_kernel, tm=tm, nk=nk, n_lat=n_lat)
    return pl.pallas_call(
        kern,
        grid=(rows // tm, nk),
        in_specs=[pl.BlockSpec((tm, tk), lambda i, k: (i, k)),
                  pl.BlockSpec((tk, d), lambda i, k: (k, 0)),
                  pl.BlockSpec((tm, d), lambda i, k: (i, 0)),
                  pl.BlockSpec((1, d), lambda i, k: (0, 0)),
                  pl.BlockSpec((None, SUBLANE, d), lambda i, k: (layer, 0, gt_idx))],
        out_specs=pl.BlockSpec((tm, d), lambda i, k: (i, 0)),
        out_shape=jax.ShapeDtypeStruct((rows, d), F32),
        scratch_shapes=[pltpu.VMEM((tm, d), F32)],
        compiler_params=_cparams(2),
        name=name,
    )(a, w_bf16, x, g, mod)


def _conv_kernel(*refs, tt, n_lat, n_tot, rope):
    if rope:
        prev_ref, cur_ref, next_ref, w_ref, b_ref, cos_ref, sin_ref, o_ref, ext_ref = refs
    else:
        prev_ref, cur_ref, next_ref, w_ref, b_ref, o_ref, ext_ref = refs
    s0 = pl.program_id(0) * tt
    first = jnp.logical_or(s0 == 0, s0 == n_lat)
    last = jnp.logical_or(s0 + tt == n_lat, s0 + tt == n_tot)
    ext_ref[0:SUBLANE, :] = jnp.where(first, 0.0, prev_ref[...])
    ext_ref[SUBLANE:SUBLANE + tt, :] = cur_ref[...]
    ext_ref[SUBLANE + tt:2 * SUBLANE + tt, :] = jnp.where(last, 0.0, next_ref[...])
    tc = cur_ref.shape[1]
    rb, cbw = 64, 512
    half = SSD_CONV // 2
    for r0 in range(0, tt, rb):
        for c0 in range(0, tc, cbw):
            acc = b_ref[:, c0:c0 + cbw] + w_ref[0:1, c0:c0 + cbw] * ext_ref[
                r0 + SUBLANE - half:r0 + SUBLANE - half + rb, c0:c0 + cbw]
            for k in range(1, SSD_CONV):
                lo = r0 + SUBLANE - half + k
                acc = acc + w_ref[k:k + 1, c0:c0 + cbw] * ext_ref[lo:lo + rb, c0:c0 + cbw]
            y = acc * _sigmoid(acc)
            if rope:
                cosv = cos_ref[r0:r0 + rb, :]
                sinv = sin_ref[r0:r0 + rb, :]
                lane = lax.broadcasted_iota(jnp.int32, (rb, LANE), 1)
                low_half = (lane % 64) < 32
                for s in range(cbw // LANE):
                    yb = y[:, s * LANE:(s + 1) * LANE]
                    sw = jnp.where(low_half, pltpu.roll(yb, LANE - 32, 1), pltpu.roll(yb, 32, 1))
                    o_ref[r0:r0 + rb, c0 + s * LANE:c0 + (s + 1) * LANE] = (
                        yb * cosv + sw * sinv).astype(o_ref.dtype)
            else:
                o_ref[r0:r0 + rb, c0:c0 + cbw] = y.astype(o_ref.dtype)


def _conv_call(xbc, conv_w, conv_b, layer, col0, ncols, cos_t, sin_t, *, n_lat, rope, name):
    n_tot = xbc.shape[0]
    tt, tc = 256, 2048
    assert n_lat % tt == 0 and n_tot % tt == 0 and col0 % tc == 0 and ncols % tc == 0
    cb = col0 // tc
    hb = tt // SUBLANE
    nhb = n_tot // SUBLANE
    kern = functools.partial(_conv_kernel, tt=tt, n_lat=n_lat, n_tot=n_tot, rope=rope)
    in_specs = [pl.BlockSpec((SUBLANE, tc), lambda i, j: (jnp.maximum(i * hb - 1, 0), cb + j)),
                pl.BlockSpec((tt, tc), lambda i, j: (i, cb + j)),
                pl.BlockSpec((SUBLANE, tc), lambda i, j: (jnp.minimum((i + 1) * hb, nhb - 1), cb + j)),
                pl.BlockSpec((None, SSD_CONV, tc), lambda i, j: (layer, 0, cb + j)),
                pl.BlockSpec((None, 1, tc), lambda i, j: (layer, 0, cb + j))]
    args = [xbc, xbc, xbc, conv_w, conv_b.reshape(conv_b.shape[0], 1, -1)]
    if rope:
        in_specs += [pl.BlockSpec((tt, LANE), lambda i, j: (i, 0)),
                     pl.BlockSpec((tt, LANE), lambda i, j: (i, 0))]
        args += [cos_t, sin_t]
    return pl.pallas_call(
        kern,
        grid=(n_tot // tt, ncols // tc),
        in_specs=in_specs,
        out_specs=pl.BlockSpec((tt, tc), lambda i, j: (i, j)),
        out_shape=jax.ShapeDtypeStruct((n_tot, ncols), BF16),
        scratch_shapes=[pltpu.VMEM((tt + 2 * SUBLANE, tc), F32)],
        compiler_params=_cparams(2),
        name=name,
    )(*args)


def _dt_kernel(raw_ref, bias_ref, alog_ref, cs_ref, w_ref, ecs_ref, cst_ref, dtt_ref):
    raw = raw_ref[...] + bias_ref[...]
    dt = jnp.maximum(raw, 0.0) + jnp.log1p(jnp.exp(-jnp.abs(raw)))
    da_t = (dt * (-jnp.exp(alog_ref[...]))).T
    lane = lax.broadcasted_iota(jnp.int32, (LANE, SSD_CHUNK), 1)
    pre = da_t
    suf = da_t
    k = 1
    while k < SSD_CHUNK:
        pre = pre + jnp.where(lane >= k, pltpu.roll(pre, k, 1), 0.0)
        suf = suf + jnp.where(lane < SSD_CHUNK - k, pltpu.roll(suf, SSD_CHUNK - k, 1), 0.0)
        k *= 2
    row = lax.broadcasted_iota(jnp.int32, (LANE, SSD_CHUNK), 0)
    cst = jnp.where(row < SSD_HEADS, pre, suf)
    cs = cst.T
    col = lax.broadcasted_iota(jnp.int32, (1, LANE), 1)
    end = jnp.where(col < SSD_HEADS, cs[SSD_CHUNK - 1:SSD_CHUNK, :], cs[0:1, :])
    cs_ref[...] = cs
    w_ref[...] = jnp.exp(end - cs) * dt
    ecs_ref[...] = jnp.exp(cs)
    cst_ref[...] = cst
    dtt_ref[...] = dt.T


def _dt_call(dt_raw, dt_bias, a_log, layer):
    n_tot = dt_raw.shape[0]
    nch = n_tot // SSD_CHUNK
    row_major = pl.BlockSpec((SSD_CHUNK, LANE), lambda c: (c, 0))
    head_major = pl.BlockSpec((LANE, SSD_CHUNK), lambda c: (0, c))
    par = pl.BlockSpec((None, 1, LANE), lambda c: (layer, 0, 0))
    return pl.pallas_call(
        _dt_kernel,
        grid=(nch,),
        in_specs=[row_major, par, par],
        out_specs=[row_major, row_major, row_major, head_major, head_major],
        out_shape=[jax.ShapeDtypeStruct((n_tot, LANE), F32)] * 3
        + [jax.ShapeDtypeStruct((LANE, n_tot), F32)] * 2,
        compiler_params=_cparams(1),
        name="ssd_dt",
    )(dt_raw, dt_bias.reshape(-1, 1, LANE), a_log.reshape(-1, 1, LANE))


def _split_cat(v, parts):
    out = []
    rem = v
    for _ in range(parts):
        p = rem.astype(BF16)
        out.append(p)
        rem = rem - p.astype(F32)
    return jnp.concatenate(out, axis=1)


def _ssd_kernel(*refs, bwd):
    if bwd:
        (x_ref, b_ref, c_ref, cst_ref, dtt_ref, cs_ref, w_ref, ecs_ref, sel_ref, sel8_ref,
         yf_ref, z_ref, dsk_ref, nw_ref, o_ref, h_ref) = refs
    else:
        (x_ref, b_ref, c_ref, cst_ref, dtt_ref, cs_ref, w_ref, ecs_ref, sel_ref, sel8_ref,
         o_ref, h_ref) = refs

    @pl.when(pl.program_id(1) == 0)
    def _():
        h_ref[...] = jnp.zeros_like(h_ref)

    x = x_ref[...]
    bm = b_ref[...]
    cm = c_ref[...]
    sel = sel_ref[...]
    sel8 = sel8_ref[...]
    w_x = _dot(_split_cat(w_ref[...], 2), jnp.concatenate([sel, sel], axis=0))
    e_x = _dot(_split_cat(ecs_ref[...], 2), jnp.concatenate([sel, sel], axis=0))
    cs8 = _dot(_split_cat(cs_ref[...], 3), jnp.concatenate([sel8, sel8, sel8], axis=0))

    cb = _dot_nt(cm, bm)
    h = h_ref[...]
    y_off = _dot(cm, h.astype(BF16)) * e_x
    xw = (x.astype(F32) * w_x).astype(BF16)
    bt = bm.astype(F32).T.astype(BF16)
    exit_row = 0 if bwd else SSD_CHUNK - 1
    h_ref[...] = h * e_x[exit_row:exit_row + 1, :] + _dot(bt, xw)

    li = lax.broadcasted_iota(jnp.int32, (SSD_CHUNK, SSD_CHUNK), 0)
    si = lax.broadcasted_iota(jnp.int32, (SSD_CHUNK, SSD_CHUNK), 1)
    mask = (li <= si) if bwd else (li >= si)
    lane = lax.broadcasted_iota(jnp.int32, (SSD_CHUNK, LANE), 1)
    cst = cst_ref[...]
    dtt = dtt_ref[...]
    ys = []
    for j in range(SSD_HPG // 2):
        ms = []
        for e in (2 * j, 2 * j + 1):
            diff = cs8[:, e:e + 1] - cst[e:e + 1, :]
            m = jnp.where(mask, jnp.exp(diff), 0.0) * (cb * dtt[e:e + 1, :])
            ms.append(m.astype(BF16))
        xp = x[:, j * LANE:(j + 1) * LANE]
        zero = jnp.zeros_like(xp)
        x2 = jnp.concatenate([jnp.where(lane < SSD_HEADDIM, xp, zero),
                              jnp.where(lane >= SSD_HEADDIM, xp, zero)], axis=0)
        ys.append(_dot(jnp.concatenate(ms, axis=1), x2))
    y = y_off + jnp.concatenate(ys, axis=1)

    if bwd:
        yt = y + yf_ref[...] + dsk_ref[...] * x.astype(F32)
        zz = z_ref[...].astype(F32)
        g = yt * (zz * _sigmoid(zz))
        gs = g * lax.rsqrt(jnp.mean(g * g, axis=-1, keepdims=True) + EPS)
        o_ref[...] = (gs * nw_ref[...]).astype(o_ref.dtype)
    else:
        o_ref[...] = y


def _ssd_call(xs, bc, dtp, sel, sel8, *, n_lat, bwd, yf=None, z=None, dsk=None, nw=None):
    cs, w, ecs, cst, dtt = dtp
    n_tot = xs.shape[0]
    ncl = n_lat // SSD_CHUNK
    nch = n_tot // SSD_CHUNK
    ncc = nch - ncl
    d = 1 if bwd else 0
    if bwd:
        def cid(s):
            return nch - 1 - s
    else:
        def cid(s):
            return jnp.where(s < ncc, ncl + s, s - ncc)
    tile = pl.BlockSpec((SSD_CHUNK, GROUP_W), lambda g, s: (cid(s), g))
    full = pl.BlockSpec((SSD_CHUNK, LANE), lambda g, s: (cid(s), 0))
    heads = pl.BlockSpec((SSD_HPG, SSD_CHUNK), lambda g, s: (d * SSD_GROUPS + g, cid(s)))
    in_specs = [tile,
                pl.BlockSpec((SSD_CHUNK, SSD_STATE), lambda g, s: (cid(s), g)),
                pl.BlockSpec((SSD_CHUNK, SSD_STATE), lambda g, s: (cid(s), SSD_GROUPS + g)),
                heads, heads, full, full, full,
                pl.BlockSpec((None, None, LANE, GROUP_W), lambda g, s: (d, g, 0, 0)),
                pl.BlockSpec((None, None, LANE, LANE), lambda g, s: (d, g, 0, 0))]
    args = [xs, bc, bc, cst, dtt, cs, w, ecs, sel, sel8]
    if bwd:
        in_specs += [tile, tile,
                     pl.BlockSpec((1, GROUP_W), lambda g, s: (0, g)),
                     pl.BlockSpec((1, GROUP_W), lambda g, s: (0, g))]
        args += [yf, z, dsk, nw]
    return pl.pallas_call(
        functools.partial(_ssd_kernel, bwd=bwd),
        grid=(SSD_GROUPS, nch),
        in_specs=in_specs,
        out_specs=tile,
        out_shape=jax.ShapeDtypeStruct((n_tot, D_INNER), BF16 if bwd else F32),
        scratch_shapes=[pltpu.VMEM((SSD_STATE, GROUP_W), F32)],
        compiler_params=_cparams(2),
        name="ssd_bwd" if bwd else "ssd_fwd",
    )(*args)


def _na_kernel(q_ref, k_ref, v_ref, bias_ref, o_ref, *, n_lat, n_ctx):
    rows = n_lat // GRID_W
    scale = NA_HEAD_DIM ** -0.5
    nk = NA_WIN_H * GRID_W
    kc = k_ref[n_lat:n_lat + n_ctx, :]
    vc = v_ref[n_lat:n_lat + n_ctx, :]

    def softmax_pv(parts):
        m = parts[0][0].max(axis=-1, keepdims=True)
        for s, _ in parts[1:]:
            m = jnp.maximum(m, s.max(axis=-1, keepdims=True))
        den = 0.0
        acc = 0.0
        for s, v in parts:
            p = jnp.exp(s - m)
            den = den + p.sum(axis=-1, keepdims=True)
            acc = acc + _dot(p.astype(BF16), v)
        return acc / den

    def body(r, carry):
        rs = jnp.clip(r - NA_WIN_H // 2, 0, rows - NA_WIN_H)
        q0 = pl.multiple_of(r * GRID_W, GRID_W)
        k0 = pl.multiple_of(rs * GRID_W, GRID_W)
        q = q_ref[pl.ds(q0, GRID_W), :]
        s_loc = _dot_nt(q, k_ref[pl.ds(k0, nk), :]) * scale + bias_ref[rs - r + NA_WIN_H - 1]
        s_ctx = _dot_nt(q, kc) * scale
        o = softmax_pv([(s_loc, v_ref[pl.ds(k0, nk), :]), (s_ctx, vc)])
        o_ref[pl.ds(q0, GRID_W), :] = o.astype(o_ref.dtype)
        return carry

    lax.fori_loop(0, rows, body, 0)
    qc = q_ref[n_lat:n_lat + n_ctx, :]
    oc = softmax_pv([(_dot_nt(qc, kc) * scale, vc)])
    o_ref[n_lat:n_lat + n_ctx, :] = oc.astype(o_ref.dtype)


def _na_call(qkvg, bias, *, n_lat):
    n_tot = qkvg.shape[0]
    n_ctx = n_tot - n_lat
    nk = NA_WIN_H * GRID_W

    def col(base):
        return pl.BlockSpec((n_tot, NA_HEAD_DIM), lambda h: (0, base + h))
    return pl.pallas_call(
        functools.partial(_na_kernel, n_lat=n_lat, n_ctx=n_ctx),
        grid=(NA_HEADS,),
        in_specs=[col(0), col(NA_HEADS), col(2 * NA_HEADS),
                  pl.BlockSpec((None, NA_WIN_H, GRID_W, nk), lambda h: (h, 0, 0, 0))],
        out_specs=col(0),
        out_shape=jax.ShapeDtypeStruct((n_tot, D_MODEL), BF16),
        compiler_params=_cparams(1),
        name="na_attn",
    )(qkvg, qkvg, qkvg, bias)


def _na_bias_table(rpb):
    c = np.arange(GRID_W)[:, None]
    kc = np.arange(GRID_W)[None, :]
    cstart = np.clip(c - NA_WIN_W // 2, 0, GRID_W - NA_WIN_W)
    ok = (kc >= cstart) & (kc < cstart + NA_WIN_W)
    dcol = np.clip(kc - c, -(NA_WIN_W - 1), NA_WIN_W - 1) + NA_WIN_W - 1
    drow = np.arange(NA_WIN_H)[:, None] + np.arange(NA_WIN_H)[None, :]
    t = rpb[:, drow[:, None, :, None], dcol[None, :, None, :]]
    t = jnp.where(ok[None, None, :, None, :], t, -jnp.inf)
    return t.reshape(rpb.shape[0], NA_WIN_H, GRID_W, NA_WIN_H * GRID_W).astype(F32)


def _gate_kernel(gn_ref, o_ref_in, ga_ref, gb_ref, w1_ref, w2_ref, out_ref):
    y1 = _dot(gn_ref[...], w1_ref[...].astype(BF16))
    y2 = _dot(o_ref_in[...], w2_ref[...].astype(BF16))
    out = _sigmoid(ga_ref[...].astype(F32)) * y1 + _sigmoid(gb_ref[...].astype(F32)) * y2
    out_ref[...] = out.astype(out_ref.dtype)


def _gate_call(gn, o_na, qkvg, w_ssd_o, w_na_o, layer, *, rows):
    tm = _pick_tile(rows, 1056)
    tn = 256
    ga0 = 3 * D_MODEL // tn
    gb0 = 4 * D_MODEL // tn
    return pl.pallas_call(
        _gate_kernel,
        grid=(rows // tm, D_MODEL // tn),
        in_specs=[pl.BlockSpec((tm, D_INNER), lambda i, j: (i, 0)),
                  pl.BlockSpec((tm, D_MODEL), lambda i, j: (i, 0)),
                  pl.BlockSpec((tm, tn), lambda i, j: (i, ga0 + j)),
                  pl.BlockSpec((tm, tn), lambda i, j: (i, gb0 + j)),
                  pl.BlockSpec((None, D_INNER, tn), lambda i, j: (layer, 0, j)),
                  pl.BlockSpec((None, D_MODEL, tn), lambda i, j: (layer, 0, j))],
        out_specs=pl.BlockSpec((tm, tn), lambda i, j: (i, j)),
        out_shape=jax.ShapeDtypeStruct((rows, D_MODEL), BF16),
        compiler_params=_cparams(2),
        name="gate_merge",
    )(gn, o_na, qkvg, qkvg, w_ssd_o, w_na_o)


def _rope_tables(n_lat, n_ctx):
    t = jnp.arange(n_lat, dtype=jnp.int32)
    n_ax = SSD_STATE // 2
    inv = ROPE_BASE ** (-jnp.arange(0, n_ax, 2, dtype=F32) / n_ax)
    ang_r = (t // GRID_W).astype(F32)[:, None] * inv
    ang_c = (t % GRID_W).astype(F32)[:, None] * inv
    cos_l = jnp.concatenate([jnp.cos(ang_r)] * 2 + [jnp.cos(ang_c)] * 2, axis=1)
    sin_l = jnp.concatenate([-jnp.sin(ang_r), jnp.sin(ang_r), -jnp.sin(ang_c), jnp.sin(ang_c)], axis=1)
    cos_t = jnp.concatenate([cos_l, jnp.ones((n_ctx, LANE), F32)], axis=0)
    sin_t = jnp.concatenate([sin_l, jnp.zeros((n_ctx, LANE), F32)], axis=0)
    return cos_t, sin_t


def _head_selectors():
    sel = np.zeros((2, SSD_GROUPS, LANE, GROUP_W), np.float32)
    sel8 = np.zeros((2, SSD_GROUPS, LANE, LANE), np.float32)
    for d in range(2):
        for g in range(SSD_GROUPS):
            for e in range(SSD_HPG):
                hd = d * SSD_HEADS + g * SSD_HPG + e
                sel[d, g, hd, e * SSD_HEADDIM:(e + 1) * SSD_HEADDIM] = 1.0
                sel8[d, g, hd, e] = 1.0
    return jnp.asarray(sel, BF16), jnp.asarray(sel8, BF16)


def kernel(x, c, ctx, c_ctx, w_ada, b_ada, g_pre_mix, g_post_mix, g_pre_mlp, g_post_mlp, w_in,
           conv_w, conv_b, a_log, dt_bias, d_skip, ssd_norm, w_ssd_o, rpb, w_na_o, w_out,
           w_mlp1, w_mlp2):
    assert x.shape[0] == 1 and x.shape[2] == D_MODEL
    depth = w_ada.shape[0]
    n_lat = x.shape[1]
    n_ctx = ctx.shape[1]
    n_tot = n_lat + n_ctx
    assert n_lat % (GRID_W * NA_WIN_H) == 0 and n_ctx % 256 == 0 and n_lat % 256 == 0

    stream = jnp.concatenate([x[0], ctx[0]], axis=0)
    cc = jnp.zeros((SUBLANE, D_MODEL), F32).at[0].set(c[0]).at[1].set(c_ctx)
    mod = _ada_call(cc, w_ada, b_ada)
    cos_t, sin_t = _rope_tables(n_lat, n_ctx)
    sel, sel8 = _head_selectors()
    w_qkvg = w_in[:, :, OFF_QKVG:]
    w_out_b = w_out.astype(BF16)
    w_mlp2_b = w_mlp2.astype(BF16)
    dsk_all = jnp.repeat(d_skip[:, 0] + d_skip[:, 1], SSD_HEADDIM, axis=-1)

    for l in range(depth):
        last = l == depth - 1
        rows_out = n_lat if last else n_tot
        g1 = g_pre_mix[l].reshape(1, -1)
        inproj = functools.partial(_norm_mm_call, stream, g1, mod, l, 0, 1, rows=n_tot, n_lat=n_lat)
        z = inproj(w_in, OFF_Z, D_INNER, 512, BF16, name="inproj_z")
        xbc = inproj(w_in, OFF_XBC, CONV_DIM, 512, F32, name="inproj_xbc")
        dt_raw = inproj(w_in, OFF_DT, 2 * SSD_HEADS, LANE, F32, name="inproj_dt")
        qkvg = inproj(w_qkvg, 0, QKVG_DIM, 512, BF16, name="inproj_qkvg")

        xs = _conv_call(xbc, conv_w, conv_b, l, 0, D_INNER, None, None, n_lat=n_lat, rope=False,
                        name="conv_x")
        bc = _conv_call(xbc, conv_w, conv_b, l, D_INNER, BC_DIM, cos_t, sin_t, n_lat=n_lat,
                        rope=True, name="conv_bc")
        dtp = _dt_call(dt_raw, dt_bias, a_log, l)
        y_f = _ssd_call(xs, bc, dtp, sel, sel8, n_lat=n_lat, bwd=False)
        gn = _ssd_call(xs, bc, dtp, sel, sel8, n_lat=n_lat, bwd=True, yf=y_f, z=z,
                       dsk=dsk_all[l].reshape(1, -1), nw=ssd_norm[l].reshape(1, -1))

        o_na = _na_call(qkvg, _na_bias_table(rpb[l]), n_lat=n_lat)
        merged = _gate_call(gn, o_na, qkvg, w_ssd_o, w_na_o, l, rows=rows_out)
        stream_mid = _mm_res_call(merged, w_out_b[l], stream, g_post_mix[l].reshape(1, -1), mod, l, 2,
                                  D_MODEL, rows=rows_out, n_lat=n_lat, name="out_proj")
        hidden = _norm_mm_call(stream_mid, g_pre_mlp[l].reshape(1, -1), mod, l, 3, 4, w_mlp1, 0, D_FF,
                               512, BF16, rows=rows_out, n_lat=n_lat, relu2=True, name="mlp_up")
        stream = _mm_res_call(hidden, w_mlp2_b[l], stream_mid, g_post_mlp[l].reshape(1, -1), mod, l, 5,
                              1024, rows=rows_out, n_lat=n_lat, name="mlp_down")
    return stream[None]
```

```python
import functools

import numpy as np
import jax
import jax.numpy as jnp
from jax import lax
from jax.experimental import pallas as pl
from jax.experimental.pallas import tpu as pltpu

F32 = jnp.float32
BF16 = jnp.bfloat16

EPS = 1e-6
LOG2E = 1.4426950408889634
ROPE_BASE = 10000.0
GRID_W = 64
D_MODEL = 2048
D_INNER = 2 * D_MODEL
SSD_HEADDIM = 64
SSD_HEADS = D_INNER // SSD_HEADDIM
SSD_GROUPS = 8
SSD_HPG = SSD_HEADS // SSD_GROUPS
SSD_STATE = 128
SSD_CONV = 5
SSD_CHUNK = 128
GROUP_W = SSD_HPG * SSD_HEADDIM
BC_DIM = 2 * SSD_GROUPS * SSD_STATE
CONV_DIM = D_INNER + BC_DIM
NA_HEAD_DIM = 128
NA_HEADS = D_MODEL // NA_HEAD_DIM
NA_WIN_H = 8
NA_WIN_W = 16
D_FF = 4 * D_MODEL
OFF_Z = 0
OFF_XBC = D_INNER
OFF_DT = OFF_XBC + CONV_DIM
OFF_QKVG = OFF_DT + 2 * SSD_HEADS
QKVG_DIM = 5 * D_MODEL

LANE = 128
SUBLANE = 8
BF16_ROWS = 16
VMEM_LIMIT_MB = 56


def _cparams(n_axes, vmem_mb=VMEM_LIMIT_MB):
    return pltpu.CompilerParams(dimension_semantics=("arbitrary",) * n_axes,
                                vmem_limit_bytes=vmem_mb * 1024 * 1024)


def _pick_tile(rows, cap):
    best = None
    for d in range(BF16_ROWS, cap + 1, BF16_ROWS):
        if rows % d == 0:
            best = d
    assert best is not None, (rows, cap)
    return best


def _row_chunk(tm):
    return _pick_tile(tm, 256)


def _sigmoid(v):
    return 1.0 / (1.0 + jnp.exp(-v))


def _dot(a, b):
    return jnp.dot(a, b, preferred_element_type=F32)


def _dot_nt(a, b):
    return lax.dot_general(a, b, (((1,), (1,)), ((), ())), preferred_element_type=F32)


def _select_rows(ref, row0, rc, n_lat):
    rows = row0 + lax.broadcasted_iota(jnp.int32, (rc, 1), 0)
    return jnp.where(rows >= n_lat, ref[1:2, :], ref[0:1, :])


def _ada_kernel(c_ref, w_ref, b_ref, o_ref):
    cv = c_ref[...]
    s = cv * _sigmoid(cv)
    o_ref[...] = _dot(s.astype(BF16), w_ref[...].astype(BF16)) + b_ref[...]


def _ada_call(cc, w_ada, b_ada):
    depth, d, n6 = w_ada.shape
    tn = 1024
    return pl.pallas_call(
        _ada_kernel,
        grid=(depth, n6 // tn),
        in_specs=[pl.BlockSpec((SUBLANE, d), lambda l, j: (0, 0)),
                  pl.BlockSpec((None, d, tn), lambda l, j: (l, 0, j)),
                  pl.BlockSpec((None, 1, tn), lambda l, j: (l, 0, j))],
        out_specs=pl.BlockSpec((None, SUBLANE, tn), lambda l, j: (l, 0, j)),
        out_shape=jax.ShapeDtypeStruct((depth, SUBLANE, n6), F32),
        compiler_params=_cparams(2),
        name="ada",
    )(cc, w_ada, b_ada.reshape(depth, 1, n6))


def _norm_mm_kernel(x_ref, g_ref, sh_ref, sc_ref, w_ref, o_ref, u_ref, *, tm, n_lat, relu2):
    i = pl.program_id(0)
    rc = _row_chunk(tm)

    @pl.when(pl.program_id(1) == 0)
    def _():
        def body(k, carry):
            r0 = pl.multiple_of(k * rc, rc)
            xv = x_ref[pl.ds(r0, rc), :]
            ms = jnp.mean(xv * xv, axis=-1, keepdims=True)
            y = xv * lax.rsqrt(ms + EPS) * g_ref[...]
            sc = _select_rows(sc_ref, i * tm + r0, rc, n_lat)
            sh = _select_rows(sh_ref, i * tm + r0, rc, n_lat)
            u_ref[pl.ds(r0, rc), :] = (y * (1.0 + sc) + sh).astype(BF16)
            return carry
        lax.fori_loop(0, tm // rc, body, 0)

    acc = _dot(u_ref[...], w_ref[...].astype(BF16))
    if relu2:
        acc = jnp.square(jnp.maximum(acc, 0.0))
    o_ref[...] = acc.astype(o_ref.dtype)


def _norm_mm_call(x, g, mod, layer, sh_idx, sc_idx, w, col0, ncols, tn, out_dtype, *, rows, n_lat,
                  relu2=False, name):
    d = x.shape[1]
    tm = _pick_tile(rows, 1056)
    assert col0 % tn == 0 and ncols % tn == 0
    cb = col0 // tn
    kern = functools.partial(_norm_mm_kernel, tm=tm, n_lat=n_lat, relu2=relu2)
    return pl.pallas_call(
        kern,
        grid=(rows // tm, ncols // tn),
        in_specs=[pl.BlockSpec((tm, d), lambda i, j: (i, 0)),
                  pl.BlockSpec((1, d), lambda i, j: (0, 0)),
                  pl.BlockSpec((None, SUBLANE, d), lambda i, j: (layer, 0, sh_idx)),
                  pl.BlockSpec((None, SUBLANE, d), lambda i, j: (layer, 0, sc_idx)),
                  pl.BlockSpec((None, d, tn), lambda i, j: (layer, 0, cb + j))],
        out_specs=pl.BlockSpec((tm, tn), lambda i, j: (i, j)),
        out_shape=jax.ShapeDtypeStruct((rows, ncols), out_dtype),
        scratch_shapes=[pltpu.VMEM((tm, d), BF16)],
        compiler_params=_cparams(2),
        name=name,
    )(x, g, mod, mod, w)


def _mm_res_kernel(a_ref, w_ref, x_ref, g_ref, gt_ref, o_ref, acc_ref, *, tm, nk, n_lat):
    i = pl.program_id(0)
    k = pl.program_id(1)
    prod = _dot(a_ref[...], w_ref[...])

    @pl.when(k == 0)
    def _():
        acc_ref[...] = prod

    @pl.when(k > 0)
    def _():
        acc_ref[...] += prod

    rc = _row_chunk(tm)

    @pl.when(k == nk - 1)
    def _():
        def body(q, carry):
            r0 = pl.multiple_of(q * rc, rc)
            y = acc_ref[pl.ds(r0, rc), :]
            ms = jnp.mean(y * y, axis=-1, keepdims=True)
            r = y * lax.rsqrt(ms + EPS) * g_ref[...]
            gt = _select_rows(gt_ref, i * tm + r0, rc, n_lat)
            o_ref[pl.ds(r0, rc), :] = x_ref[pl.ds(r0, rc), :] + gt * r
            return carry
        lax.fori_loop(0, tm // rc, body, 0)


def _mm_res_call(a, w_bf16, x, g, mod, layer, gt_idx, tk, *, rows, n_lat, name):
    kdim, d = w_bf16.shape
    tm = _pick_tile(rows, 528)
    nk = kdim // tk
    kern = functools.partial(_mm_res_kernel, tm=tm, nk=nk, n_lat=n_lat)
    return pl.pallas_call(
        kern,
        grid=(rows // tm, nk),
        in_specs=[pl.BlockSpec((tm, tk), lambda i, k: (i, k)),
                  pl.BlockSpec((tk, d), lambda i, k: (k, 0)),
                  pl.BlockSpec((tm, d), lambda i, k: (i, 0)),
                  pl.BlockSpec((1, d), lambda i, k: (0, 0)),
                  pl.BlockSpec((None, SUBLANE, d), lambda i, k: (layer, 0, gt_idx))],
        out_specs=pl.BlockSpec((tm, d), lambda i, k: (i, 0)),
        out_shape=jax.ShapeDtypeStruct((rows, d), F32),
        scratch_shapes=[pltpu.VMEM((tm, d), F32)],
        compiler_params=_cparams(2),
        name=name,
    )(a, w_bf16, x, g, mod)


def _conv_kernel(*refs, tt, n_lat, n_tot, rope):
    if rope:
        prev_ref, cur_ref, next_ref, w_ref, b_ref, cos_ref, sin_ref, o_ref, ext_ref = refs
    else:
        prev_ref, cur_ref, next_ref, w_ref, b_ref, o_ref, ext_ref = refs
    s0 = pl.program_id(0) * tt
    first = jnp.logical_or(s0 == 0, s0 == n_lat)
    last = jnp.logical_or(s0 + tt == n_lat, s0 + tt == n_tot)
    ext_ref[0:SUBLANE, :] = jnp.where(first, 0.0, prev_ref[...])
    ext_ref[SUBLANE:SUBLANE + tt, :] = cur_ref[...]
    ext_ref[SUBLANE + tt:2 * SUBLANE + tt, :] = jnp.where(last, 0.0, next_ref[...])
    tc = cur_ref.shape[1]
    rb, cbw = 64, 512
    half = SSD_CONV // 2
    for r0 in range(0, tt, rb):
        for c0 in range(0, tc, cbw):
            acc = b_ref[:, c0:c0 + cbw] + w_ref[0:1, c0:c0 + cbw] * ext_ref[
                r0 + SUBLANE - half:r0 + SUBLANE - half + rb, c0:c0 + cbw]
            for k in range(1, SSD_CONV):
                lo = r0 + SUBLANE - half + k
                acc = acc + w_ref[k:k + 1, c0:c0 + cbw] * ext_ref[lo:lo + rb, c0:c0 + cbw]
            y = acc * _sigmoid(acc)
            if rope:
                cosv = cos_ref[r0:r0 + rb, :]
                sinv = sin_ref[r0:r0 + rb, :]
                lane = lax.broadcasted_iota(jnp.int32, (rb, LANE), 1)
                low_half = (lane % 64) < 32
                for s in range(cbw // LANE):
                    yb = y[:, s * LANE:(s + 1) * LANE]
                    sw = jnp.where(low_half, pltpu.roll(yb, LANE - 32, 1), pltpu.roll(yb, 32, 1))
                    o_ref[r0:r0 + rb, c0 + s * LANE:c0 + (s + 1) * LANE] = (
                        yb * cosv + sw * sinv).astype(o_ref.dtype)
            else:
                o_ref[r0:r0 + rb, c0:c0 + cbw] = y.astype(o_ref.dtype)


def _conv_call(xbc, conv_w, conv_b, layer, col0, ncols, cos_t, sin_t, *, n_lat, rope, name):
    n_tot = xbc.shape[0]
    tt, tc = 256, 2048
    assert n_lat % tt == 0 and n_tot % tt == 0 and col0 % tc == 0 and ncols % tc == 0
    cb = col0 // tc
    hb = tt // SUBLANE
    nhb = n_tot // SUBLANE
    kern = functools.partial(_conv_kernel, tt=tt, n_lat=n_lat, n_tot=n_tot, rope=rope)
    in_specs = [pl.BlockSpec((SUBLANE, tc), lambda i, j: (jnp.maximum(i * hb - 1, 0), cb + j)),
                pl.BlockSpec((tt, tc), lambda i, j: (i, cb + j)),
                pl.BlockSpec((SUBLANE, tc), lambda i, j: (jnp.minimum((i + 1) * hb, nhb - 1), cb + j)),
                pl.BlockSpec((None, SSD_CONV, tc), lambda i, j: (layer, 0, cb + j)),
                pl.BlockSpec((None, 1, tc), lambda i, j: (layer, 0, cb + j))]
    args = [xbc, xbc, xbc, conv_w, conv_b.reshape(conv_b.shape[0], 1, -1)]
    if rope:
        in_specs += [pl.BlockSpec((tt, LANE), lambda i, j: (i, 0)),
                     pl.BlockSpec((tt, LANE), lambda i, j: (i, 0))]
        args += [cos_t, sin_t]
    return pl.pallas_call(
        kern,
        grid=(n_tot // tt, ncols // tc),
        in_specs=in_specs,
        out_specs=pl.BlockSpec((tt, tc), lambda i, j: (i, j)),
        out_shape=jax.ShapeDtypeStruct((n_tot, ncols), BF16),
        scratch_shapes=[pltpu.VMEM((tt + 2 * SUBLANE, tc), F32)],
        compiler_params=_cparams(2),
        name=name,
    )(*args)


def _dt_kernel(raw_ref, bias_ref, alog_ref, cs_ref, w_ref, ecs_ref, cst_ref, dtt_ref):
    raw = raw_ref[...] + bias_ref[...]
    dt = jnp.maximum(raw, 0.0) + jnp.log1p(jnp.exp(-jnp.abs(raw)))
    da_t = (dt * (-jnp.exp(alog_ref[...]))).T
    lane = lax.broadcasted_iota(jnp.int32, (LANE, SSD_CHUNK), 1)
    pre = da_t
    suf = da_t
    k = 1
    while k < SSD_CHUNK:
        pre = pre + jnp.where(lane >= k, pltpu.roll(pre, k, 1), 0.0)
        suf = suf + jnp.where(lane < SSD_CHUNK - k, pltpu.roll(suf, SSD_CHUNK - k, 1), 0.0)
        k *= 2
    row = lax.broadcasted_iota(jnp.int32, (LANE, SSD_CHUNK), 0)
    cst = jnp.where(row < SSD_HEADS, pre, suf)
    cs = cst.T
    col = lax.broadcasted_iota(jnp.int32, (1, LANE), 1)
    end = jnp.where(col < SSD_HEADS, cs[SSD_CHUNK - 1:SSD_CHUNK, :], cs[0:1, :])
    cs_ref[...] = cs * LOG2E
    w_ref[...] = jnp.exp(end - cs) * dt
    ecs_ref[...] = jnp.exp(cs)
    cst_ref[...] = cst * LOG2E
    dtt_ref[...] = dt.T


def _dt_call(dt_raw, dt_bias, a_log, layer):
    n_tot = dt_raw.shape[0]
    nch = n_tot // SSD_CHUNK
    row_major = pl.BlockSpec((SSD_CHUNK, LANE), lambda c: (c, 0))
    head_major = pl.BlockSpec((LANE, SSD_CHUNK), lambda c: (0, c))
    par = pl.BlockSpec((None, 1, LANE), lambda c: (layer, 0, 0))
    return pl.pallas_call(
        _dt_kernel,
        grid=(nch,),
        in_specs=[row_major, par, par],
        out_specs=[row_major, row_major, row_major, head_major, head_major],
        out_shape=[jax.ShapeDtypeStruct((n_tot, LANE), F32)] * 3
        + [jax.ShapeDtypeStruct((LANE, n_tot), F32)] * 2,
        compiler_params=_cparams(1),
        name="ssd_dt",
    )(dt_raw, dt_bias.reshape(-1, 1, LANE), a_log.reshape(-1, 1, LANE))


def _split_cat(v, parts):
    out = []
    rem = v
    for _ in range(parts):
        p = rem.astype(BF16)
        out.append(p)
        rem = rem - p.astype(F32)
    return jnp.concatenate(out, axis=1)


SSD_GPS = 4


def _ssd_kernel(*refs, bwd):
    if bwd:
        (x_ref, b_ref, c_ref, cst_ref, dtt_ref, cs_ref, w_ref, ecs_ref, sel_ref, sel8_ref,
         yf_ref, z_ref, dsk_ref, nw_ref, o_ref, h_ref) = refs
    else:
        (x_ref, b_ref, c_ref, cst_ref, dtt_ref, cs_ref, w_ref, ecs_ref, sel_ref, sel8_ref,
         o_ref, h_ref) = refs

    @pl.when(pl.program_id(1) == 0)
    def _():
        h_ref[...] = jnp.zeros_like(h_ref)

    w_sp = _split_cat(w_ref[...], 2)
    e_sp = _split_cat(ecs_ref[...], 2)
    cs_sp = _split_cat(cs_ref[...], 3)
    li = lax.broadcasted_iota(jnp.int32, (SSD_CHUNK, SSD_CHUNK), 0)
    si = lax.broadcasted_iota(jnp.int32, (SSD_CHUNK, SSD_CHUNK), 1)
    mask = (li <= si) if bwd else (li >= si)
    lane = lax.broadcasted_iota(jnp.int32, (SSD_CHUNK, LANE), 1)
    exit_row = 0 if bwd else SSD_CHUNK - 1

    for gi in range(SSD_GPS):
        gsl = slice(gi * GROUP_W, (gi + 1) * GROUP_W)
        x = x_ref[:, gsl]
        bm = b_ref[:, gi * SSD_STATE:(gi + 1) * SSD_STATE]
        cm = c_ref[:, gi * SSD_STATE:(gi + 1) * SSD_STATE]
        sel = sel_ref[gi]
        sel8 = sel8_ref[gi]
        w_x = _dot(w_sp, jnp.concatenate([sel, sel], axis=0))
        e_x = _dot(e_sp, jnp.concatenate([sel, sel], axis=0))
        cs8 = _dot(cs_sp, jnp.concatenate([sel8, sel8, sel8], axis=0))

        cb = _dot_nt(cm, bm)
        h = h_ref[gi]
        y_off = _dot(cm, h.astype(BF16)) * e_x
        xw = (x.astype(F32) * w_x).astype(BF16)
        bt = bm.astype(F32).T.astype(BF16)
        h_ref[gi] = h * e_x[exit_row:exit_row + 1, :] + _dot(bt, xw)

        cst = cst_ref[gi * SSD_HPG:(gi + 1) * SSD_HPG, :]
        dtt = dtt_ref[gi * SSD_HPG:(gi + 1) * SSD_HPG, :]
        ys = []
        for j in range(SSD_HPG // 2):
            ms = []
            for e in (2 * j, 2 * j + 1):
                diff = cs8[:, e:e + 1] - cst[e:e + 1, :]
                m = jnp.where(mask, jnp.exp2(diff), 0.0) * (cb * dtt[e:e + 1, :])
                ms.append(m.astype(BF16))
            xp = x[:, j * LANE:(j + 1) * LANE]
            zero = jnp.zeros_like(xp)
            x2 = jnp.concatenate([jnp.where(lane < SSD_HEADDIM, xp, zero),
                                  jnp.where(lane >= SSD_HEADDIM, xp, zero)], axis=0)
            ys.append(_dot(jnp.concatenate(ms, axis=1), x2))
        y = y_off + jnp.concatenate(ys, axis=1)

        if bwd:
            yt = y + yf_ref[:, gsl] + dsk_ref[:, gsl] * x.astype(F32)
            zz = z_ref[:, gsl].astype(F32)
            g = yt * (zz * _sigmoid(zz))
            gs = g * lax.rsqrt(jnp.mean(g * g, axis=-1, keepdims=True) + EPS)
            o_ref[:, gsl] = (gs * nw_ref[:, gsl]).astype(o_ref.dtype)
        else:
            o_ref[:, gsl] = y


def _ssd_call(xs, bc, dtp, sel, sel8, *, n_lat, bwd, yf=None, z=None, dsk=None, nw=None):
    cs, w, ecs, cst, dtt = dtp
    n_tot = xs.shape[0]
    ncl = n_lat // SSD_CHUNK
    nch = n_tot // SSD_CHUNK
    ncc = nch - ncl
    d = 1 if bwd else 0
    gsteps = SSD_GROUPS // SSD_GPS
    tw = SSD_GPS * GROUP_W
    sw = SSD_GPS * SSD_STATE
    if bwd:
        def cid(s):
            return nch - 1 - s
    else:
        def cid(s):
            return jnp.where(s < ncc, ncl + s, s - ncc)
    tile = pl.BlockSpec((SSD_CHUNK, tw), lambda g, s: (cid(s), g))
    full = pl.BlockSpec((SSD_CHUNK, LANE), lambda g, s: (cid(s), 0))
    heads = pl.BlockSpec((SSD_GPS * SSD_HPG, SSD_CHUNK), lambda g, s: (d * gsteps + g, cid(s)))
    in_specs = [tile,
                pl.BlockSpec((SSD_CHUNK, sw), lambda g, s: (cid(s), g)),
                pl.BlockSpec((SSD_CHUNK, sw), lambda g, s: (cid(s), gsteps + g)),
                heads, heads, full, full, full,
                pl.BlockSpec((None, SSD_GPS, LANE, GROUP_W), lambda g, s: (d, g, 0, 0)),
                pl.BlockSpec((None, SSD_GPS, LANE, LANE), lambda g, s: (d, g, 0, 0))]
    args = [xs, bc, bc, cst, dtt, cs, w, ecs, sel, sel8]
    if bwd:
        in_specs += [tile, tile,
                     pl.BlockSpec((1, tw), lambda g, s: (0, g)),
                     pl.BlockSpec((1, tw), lambda g, s: (0, g))]
        args += [yf, z, dsk, nw]
    return pl.pallas_call(
        functools.partial(_ssd_kernel, bwd=bwd),
        grid=(gsteps, nch),
        in_specs=in_specs,
        out_specs=tile,
        out_shape=jax.ShapeDtypeStruct((n_tot, D_INNER), BF16 if bwd else F32),
        scratch_shapes=[pltpu.VMEM((SSD_GPS, SSD_STATE, GROUP_W), F32)],
        compiler_params=_cparams(2),
        name="ssd_bwd" if bwd else "ssd_fwd",
    )(*args)


NA_QROWS = 4
NA_KROWS = NA_QROWS + NA_WIN_H
NA_TOK_BLK = 128


def _na_kernel(q_ref, k_ref, v_ref, bias_ref, o_ref, qt_ref, vt_ref, *, n_lat, n_ctx):
    rows = n_lat // GRID_W
    qscale = NA_HEAD_DIM ** -0.5 * LOG2E
    nq = NA_QROWS * GRID_W
    nkeys = NA_KROWS * GRID_W
    qblk = nq // NA_TOK_BLK
    kblk = nkeys // NA_TOK_BLK
    lat_blk = n_lat // NA_TOK_BLK
    ctx_blk = n_ctx // NA_TOK_BLK

    def transpose_blocks(p, carry):
        t0 = pl.multiple_of(p * NA_TOK_BLK, NA_TOK_BLK)
        qt_ref[p] = (q_ref[pl.ds(t0, NA_TOK_BLK), :].astype(F32) * qscale).T.astype(BF16)
        vt_ref[p] = v_ref[pl.ds(t0, NA_TOK_BLK), :].astype(F32).T.astype(BF16)
        return carry

    lax.fori_loop(0, lat_blk + ctx_blk, transpose_blocks, 0, unroll=2)

    kc = k_ref[n_lat:n_lat + n_ctx, :]
    vtc = jnp.concatenate([vt_ref[lat_blk + b] for b in range(ctx_blk)], axis=1)

    def softmax_pv(parts):
        m = parts[0][0].max(axis=0, keepdims=True)
        for s, _ in parts[1:]:
            m = jnp.maximum(m, s.max(axis=0, keepdims=True))
        den = 0.0
        acc = 0.0
        for s, vt in parts:
            p = jnp.exp2(s - m)
            den = den + p.sum(axis=0, keepdims=True)
            acc = acc + _dot(vt, p.astype(BF16))
        return acc / den

    def body(t, carry):
        r0 = t * NA_QROWS
        ws = jnp.clip(r0 - NA_WIN_H // 2, 0, rows - NA_WIN_H)
        variant = jnp.where(r0 == 0, 0, jnp.where(r0 == rows - NA_QROWS, 2, 1))
        q0 = pl.multiple_of(r0 * GRID_W, nq)
        k0 = pl.multiple_of(ws * GRID_W, nq)
        kb0 = ws * GRID_W // NA_TOK_BLK
        qt = jnp.concatenate([qt_ref[t * qblk + b] for b in range(qblk)], axis=1)
        vt = jnp.concatenate([vt_ref[kb0 + b] for b in range(kblk)], axis=1)
        s_loc = _dot(k_ref[pl.ds(k0, nkeys), :], qt) + bias_ref[variant]
        s_ctx = _dot(kc, qt)
        ot = softmax_pv([(s_loc, vt), (s_ctx, vtc)])
        o_ref[pl.ds(q0, nq), :] = ot.T.astype(o_ref.dtype)
        return carry

    lax.fori_loop(0, rows // NA_QROWS, body, 0, unroll=2)
    qtc = jnp.concatenate([qt_ref[lat_blk + b] for b in range(ctx_blk)], axis=1)
    otc = softmax_pv([(_dot(kc, qtc), vtc)])
    o_ref[n_lat:n_lat + n_ctx, :] = otc.T.astype(o_ref.dtype)


def _na_call(qkvg, bias, *, n_lat):
    n_tot = qkvg.shape[0]
    n_ctx = n_tot - n_lat
    nblk = n_tot // NA_TOK_BLK
    assert n_ctx >= NA_QROWS * GRID_W

    def col(base):
        return pl.BlockSpec((n_tot, NA_HEAD_DIM), lambda h: (0, base + h))
    return pl.pallas_call(
        functools.partial(_na_kernel, n_lat=n_lat, n_ctx=n_ctx),
        grid=(NA_HEADS,),
        in_specs=[col(0), col(NA_HEADS), col(2 * NA_HEADS),
                  pl.BlockSpec((None, 3, NA_KROWS * GRID_W, NA_QROWS * GRID_W), lambda h: (h, 0, 0, 0))],
        out_specs=col(0),
        out_shape=jax.ShapeDtypeStruct((n_tot, D_MODEL), BF16),
        scratch_shapes=[pltpu.VMEM((nblk, NA_HEAD_DIM, NA_TOK_BLK), BF16),
                        pltpu.VMEM((nblk, NA_HEAD_DIM, NA_TOK_BLK), BF16)],
        compiler_params=_cparams(1),
        name="na_attn",
    )(qkvg, qkvg, qkvg, bias)


def _na_bias_table(rpb):
    nh = rpb.shape[0]
    c = np.arange(GRID_W)[:, None]
    kc = np.arange(GRID_W)[None, :]
    cstart = np.clip(c - NA_WIN_W // 2, 0, GRID_W - NA_WIN_W)
    col_ok = (kc >= cstart) & (kc < cstart + NA_WIN_W)
    dcol = np.clip(kc - c, -(NA_WIN_W - 1), NA_WIN_W - 1) + NA_WIN_W - 1
    onehot = (dcol[None] == np.arange(2 * NA_WIN_W - 1)[:, None, None]).astype(np.float32)
    a = jnp.einsum("hdj,jck->hdck", rpb, jnp.asarray(onehot), precision=lax.Precision.HIGHEST)
    a = jnp.where(col_ok[None, None], a, -jnp.inf)
    neg = functools.partial(jnp.pad, mode="constant", constant_values=-jnp.inf)
    extra = NA_KROWS - NA_WIN_H
    groups = []
    for variant in range(3):
        per_row = []
        for i in range(NA_QROWS):
            if variant == 0:
                d0, before = NA_WIN_H - 1 - i, 0
            elif variant == 1:
                d0, before = NA_WIN_H // 2 - 1, i
            else:
                d0, before = NA_WIN_H // 2 - 1 - i, 0
            piece = a[:, d0:d0 + NA_WIN_H]
            per_row.append(neg(piece, ((0, 0), (before, extra - before), (0, 0), (0, 0))))
        groups.append(jnp.stack(per_row, axis=1))
    t = jnp.stack(groups, axis=1)
    t = jnp.transpose(t, (0, 1, 3, 5, 2, 4)) * LOG2E
    return t.reshape(nh, 3, NA_KROWS * GRID_W, NA_QROWS * GRID_W).astype(F32)


def _gate_kernel(gn_ref, o_ref_in, ga_ref, gb_ref, w1_ref, w2_ref, out_ref):
    y1 = _dot(gn_ref[...], w1_ref[...].astype(BF16))
    y2 = _dot(o_ref_in[...], w2_ref[...].astype(BF16))
    out = _sigmoid(ga_ref[...].astype(F32)) * y1 + _sigmoid(gb_ref[...].astype(F32)) * y2
    out_ref[...] = out.astype(out_ref.dtype)


def _gate_call(gn, o_na, qkvg, w_ssd_o, w_na_o, layer, *, rows):
    tm = _pick_tile(rows, 1056)
    tn = 256
    ga0 = 3 * D_MODEL // tn
    gb0 = 4 * D_MODEL // tn
    return pl.pallas_call(
        _gate_kernel,
        grid=(rows // tm, D_MODEL // tn),
        in_specs=[pl.BlockSpec((tm, D_INNER), lambda i, j: (i, 0)),
                  pl.BlockSpec((tm, D_MODEL), lambda i, j: (i, 0)),
                  pl.BlockSpec((tm, tn), lambda i, j: (i, ga0 + j)),
                  pl.BlockSpec((tm, tn), lambda i, j: (i, gb0 + j)),
                  pl.BlockSpec((None, D_INNER, tn), lambda i, j: (layer, 0, j)),
                  pl.BlockSpec((None, D_MODEL, tn), lambda i, j: (layer, 0, j))],
        out_specs=pl.BlockSpec((tm, tn), lambda i, j: (i, j)),
        out_shape=jax.ShapeDtypeStruct((rows, D_MODEL), BF16),
        compiler_params=_cparams(2),
        name="gate_merge",
    )(gn, o_na, qkvg, qkvg, w_ssd_o, w_na_o)


def _rope_tables(n_lat, n_ctx):
    t = jnp.arange(n_lat, dtype=jnp.int32)
    n_ax = SSD_STATE // 2
    inv = ROPE_BASE ** (-jnp.arange(0, n_ax, 2, dtype=F32) / n_ax)
    ang_r = (t // GRID_W).astype(F32)[:, None] * inv
    ang_c = (t % GRID_W).astype(F32)[:, None] * inv
    cos_l = jnp.concatenate([jnp.cos(ang_r)] * 2 + [jnp.cos(ang_c)] * 2, axis=1)
    sin_l = jnp.concatenate([-jnp.sin(ang_r), jnp.sin(ang_r), -jnp.sin(ang_c), jnp.sin(ang_c)], axis=1)
    cos_t = jnp.concatenate([cos_l, jnp.ones((n_ctx, LANE), F32)], axis=0)
    sin_t = jnp.concatenate([sin_l, jnp.zeros((n_ctx, LANE), F32)], axis=0)
    return cos_t, sin_t


def _head_selectors():
    sel = np.zeros((2, SSD_GROUPS, LANE, GROUP_W), np.float32)
    sel8 = np.zeros((2, SSD_GROUPS, LANE, LANE), np.float32)
    for d in range(2):
        for g in range(SSD_GROUPS):
            for e in range(SSD_HPG):
                hd = d * SSD_HEADS + g * SSD_HPG + e
                sel[d, g, hd, e * SSD_HEADDIM:(e + 1) * SSD_HEADDIM] = 1.0
                sel8[d, g, hd, e] = 1.0
    return jnp.asarray(sel, BF16), jnp.asarray(sel8, BF16)


def kernel(x, c, ctx, c_ctx, w_ada, b_ada, g_pre_mix, g_post_mix, g_pre_mlp, g_post_mlp, w_in,
           conv_w, conv_b, a_log, dt_bias, d_skip, ssd_norm, w_ssd_o, rpb, w_na_o, w_out,
           w_mlp1, w_mlp2):
    assert x.shape[0] == 1 and x.shape[2] == D_MODEL
    depth = w_ada.shape[0]
    n_lat = x.shape[1]
    n_ctx = ctx.shape[1]
    n_tot = n_lat + n_ctx
    assert n_lat % (GRID_W * NA_WIN_H) == 0 and n_ctx % 256 == 0 and n_lat % 256 == 0

    stream = jnp.concatenate([x[0], ctx[0]], axis=0)
    cc = jnp.zeros((SUBLANE, D_MODEL), F32).at[0].set(c[0]).at[1].set(c_ctx)
    mod = _ada_call(cc, w_ada, b_ada)
    cos_t, sin_t = _rope_tables(n_lat, n_ctx)
    sel, sel8 = _head_selectors()
    w_qkvg = w_in[:, :, OFF_QKVG:]
    w_out_b = w_out.astype(BF16)
    w_mlp2_b = w_mlp2.astype(BF16)
    dsk_all = jnp.repeat(d_skip[:, 0] + d_skip[:, 1], SSD_HEADDIM, axis=-1)

    for l in range(depth):
        last = l == depth - 1
        rows_out = n_lat if last else n_tot
        g1 = g_pre_mix[l].reshape(1, -1)
        inproj = functools.partial(_norm_mm_call, stream, g1, mod, l, 0, 1, rows=n_tot, n_lat=n_lat)
        z = inproj(w_in, OFF_Z, D_INNER, 512, BF16, name="inproj_z")
        xbc = inproj(w_in, OFF_XBC, CONV_DIM, 512, F32, name="inproj_xbc")
        dt_raw = inproj(w_in, OFF_DT, 2 * SSD_HEADS, LANE, F32, name="inproj_dt")
        qkvg = inproj(w_qkvg, 0, QKVG_DIM, 512, BF16, name="inproj_qkvg")

        xs = _conv_call(xbc, conv_w, conv_b, l, 0, D_INNER, None, None, n_lat=n_lat, rope=False,
                        name="conv_x")
        bc = _conv_call(xbc, conv_w, conv_b, l, D_INNER, BC_DIM, cos_t, sin_t, n_lat=n_lat,
                        rope=True, name="conv_bc")
        dtp = _dt_call(dt_raw, dt_bias, a_log, l)
        y_f = _ssd_call(xs, bc, dtp, sel, sel8, n_lat=n_lat, bwd=False)
        gn = _ssd_call(xs, bc, dtp, sel, sel8, n_lat=n_lat, bwd=True, yf=y_f, z=z,
                       dsk=dsk_all[l].reshape(1, -1), nw=ssd_norm[l].reshape(1, -1))

        o_na = _na_call(qkvg, _na_bias_table(rpb[l]), n_lat=n_lat)
        merged = _gate_call(gn, o_na, qkvg, w_ssd_o, w_na_o, l, rows=rows_out)
        stream_mid = _mm_res_call(merged, w_out_b[l], stream, g_post_mix[l].reshape(1, -1), mod, l, 2,
                                  D_MODEL, rows=rows_out, n_lat=n_lat, name="out_proj")
        hidden = _norm_mm_call(stream_mid, g_pre_mlp[l].reshape(1, -1), mod, l, 3, 4, w_mlp1, 0, D_FF,
                               512, BF16, rows=rows_out, n_lat=n_lat, relu2=True, name="mlp_up")
        stream = _mm_res_call(hidden, w_mlp2_b[l], stream_mid, g_post_mlp[l].reshape(1, -1), mod, l, 5,
                              1024, rows=rows_out, n_lat=n_lat, name="mlp_down")
    return stream[None]
```

```python
import functools

import numpy as np
import jax
import jax.numpy as jnp
from jax import lax
from jax.experimental import pallas as pl
from jax.experimental.pallas import tpu as pltpu

F32 = jnp.float32
BF16 = jnp.bfloat16

EPS = 1e-6
LOG2E = 1.4426950408889634
ROPE_BASE = 10000.0
GRID_W = 64
D_MODEL = 2048
D_INNER = 2 * D_MODEL
SSD_HEADDIM = 64
SSD_HEADS = D_INNER // SSD_HEADDIM
SSD_GROUPS = 8
SSD_HPG = SSD_HEADS // SSD_GROUPS
SSD_STATE = 128
SSD_CONV = 5
SSD_CHUNK = 128
GROUP_W = SSD_HPG * SSD_HEADDIM
BC_DIM = 2 * SSD_GROUPS * SSD_STATE
CONV_DIM = D_INNER + BC_DIM
NA_HEAD_DIM = 128
NA_HEADS = D_MODEL // NA_HEAD_DIM
NA_WIN_H = 8
NA_WIN_W = 16
D_FF = 4 * D_MODEL
OFF_Z = 0
OFF_XBC = D_INNER
OFF_DT = OFF_XBC + CONV_DIM
OFF_QKVG = OFF_DT + 2 * SSD_HEADS
QKVG_DIM = 5 * D_MODEL

LANE = 128
SUBLANE = 8
BF16_ROWS = 16
VMEM_LIMIT_MB = 56


def _cparams(n_axes, vmem_mb=VMEM_LIMIT_MB):
    return pltpu.CompilerParams(dimension_semantics=("arbitrary",) * n_axes,
                                vmem_limit_bytes=vmem_mb * 1024 * 1024)


def _pick_tile(rows, cap):
    best = None
    for d in range(BF16_ROWS, cap + 1, BF16_ROWS):
        if rows % d == 0:
            best = d
    assert best is not None, (rows, cap)
    return best


def _row_chunk(tm):
    return _pick_tile(tm, 256)


def _sigmoid(v):
    return 1.0 / (1.0 + jnp.exp(-v))


def _dot(a, b):
    return jnp.dot(a, b, preferred_element_type=F32)


def _dot_nt(a, b):
    return lax.dot_general(a, b, (((1,), (1,)), ((), ())), preferred_element_type=F32)


def _select_rows(ref, row0, rc, n_lat):
    rows = row0 + lax.broadcasted_iota(jnp.int32, (rc, 1), 0)
    return jnp.where(rows >= n_lat, ref[1:2, :], ref[0:1, :])


def _ada_kernel(c_ref, w_ref, b_ref, o_ref):
    cv = c_ref[...]
    s = cv * _sigmoid(cv)
    o_ref[...] = _dot(s.astype(BF16), w_ref[...].astype(BF16)) + b_ref[...]


def _ada_call(cc, w_ada, b_ada):
    depth, d, n6 = w_ada.shape
    tn = 1024
    return pl.pallas_call(
        _ada_kernel,
        grid=(depth, n6 // tn),
        in_specs=[pl.BlockSpec((SUBLANE, d), lambda l, j: (0, 0)),
                  pl.BlockSpec((None, d, tn), lambda l, j: (l, 0, j)),
                  pl.BlockSpec((None, 1, tn), lambda l, j: (l, 0, j))],
        out_specs=pl.BlockSpec((None, SUBLANE, tn), lambda l, j: (l, 0, j)),
        out_shape=jax.ShapeDtypeStruct((depth, SUBLANE, n6), F32),
        compiler_params=_cparams(2),
        name="ada",
    )(cc, w_ada, b_ada.reshape(depth, 1, n6))


def _norm_modulate(v, g_ref, sh, sc):
    ms = jnp.mean(v * v, axis=-1, keepdims=True)
    return v * lax.rsqrt(ms + EPS) * g_ref[...] * (1.0 + sc) + sh


def _stream_kernel(x_ref, c_ref, g_ref, sh_ref, sc_ref, s_ref, u_ref, *, lat_tiles):
    i = pl.program_id(0)

    def emit(v, row):
        s_ref[...] = v
        u_ref[...] = _norm_modulate(v, g_ref, sh_ref[row:row + 1, :], sc_ref[row:row + 1, :]
                                    ).astype(u_ref.dtype)

    @pl.when(i < lat_tiles)
    def _():
        emit(x_ref[...], 0)

    @pl.when(i >= lat_tiles)
    def _():
        emit(c_ref[...], 1)


def _stream_call(x2, ctx2, g, mod):
    n_lat, d = x2.shape
    n_ctx = ctx2.shape[0]
    tr = 256
    lat_tiles = n_lat // tr
    row = pl.BlockSpec((tr, d), lambda i: (i, 0))
    return pl.pallas_call(
        functools.partial(_stream_kernel, lat_tiles=lat_tiles),
        grid=((n_lat + n_ctx) // tr,),
        in_specs=[pl.BlockSpec((tr, d), lambda i: (jnp.minimum(i, lat_tiles - 1), 0)),
                  pl.BlockSpec((tr, d), lambda i: (jnp.maximum(i - lat_tiles, 0), 0)),
                  pl.BlockSpec((1, d), lambda i: (0, 0)),
                  pl.BlockSpec((None, SUBLANE, d), lambda i: (0, 0, 0)),
                  pl.BlockSpec((None, SUBLANE, d), lambda i: (0, 0, 1))],
        out_specs=[row, row],
        out_shape=[jax.ShapeDtypeStruct((n_lat + n_ctx, d), F32),
                   jax.ShapeDtypeStruct((n_lat + n_ctx, d), BF16)],
        compiler_params=_cparams(1),
        name="stream",
    )(x2, ctx2, g, mod, mod)


def _inproj_kernel(a_ref, w_ref, wdt_ref, z_ref, xbc_ref, qkvg_ref, dt_ref, *, nz, nx):
    j = pl.program_id(1)
    a = a_ref[...]
    acc = _dot(a, w_ref[0].astype(BF16))

    @pl.when(j < nz)
    def _():
        z_ref[...] = acc.astype(z_ref.dtype)

    @pl.when(jnp.logical_and(j >= nz, j < nz + nx))
    def _():
        xbc_ref[...] = acc.astype(xbc_ref.dtype)

    @pl.when(j >= nz + nx)
    def _():
        qkvg_ref[...] = acc.astype(qkvg_ref.dtype)

    @pl.when(j == 0)
    def _():
        dt_ref[...] = _dot(a, wdt_ref[...].astype(BF16))


def _inproj_call(u, w_in, layer):
    rows, d = u.shape
    tm = _pick_tile(rows, 1408)
    tn = 512
    nz, nx, nq = D_INNER // tn, CONV_DIM // tn, QKVG_DIM // tn
    dtw = 2 * SSD_HEADS

    def wcol(i, j):
        return (layer, 0, pl.multiple_of(j * tn + jnp.where(j >= nz + nx, dtw, 0), dtw))
    return pl.pallas_call(
        functools.partial(_inproj_kernel, nz=nz, nx=nx),
        grid=(rows // tm, nz + nx + nq),
        in_specs=[pl.BlockSpec((tm, d), lambda i, j: (i, 0)),
                  pl.BlockSpec((pl.Element(1), pl.Element(d), pl.Element(tn)), wcol),
                  pl.BlockSpec((None, d, dtw), lambda i, j: (layer, 0, OFF_DT // dtw))],
        out_specs=[pl.BlockSpec((tm, tn), lambda i, j: (i, jnp.minimum(j, nz - 1))),
                   pl.BlockSpec((tm, tn), lambda i, j: (i, jnp.clip(j - nz, 0, nx - 1))),
                   pl.BlockSpec((tm, tn), lambda i, j: (i, jnp.maximum(j - nz - nx, 0))),
                   pl.BlockSpec((tm, dtw), lambda i, j: (i, 0))],
        out_shape=[jax.ShapeDtypeStruct((rows, D_INNER), BF16),
                   jax.ShapeDtypeStruct((rows, CONV_DIM), F32),
                   jax.ShapeDtypeStruct((rows, QKVG_DIM), BF16),
                   jax.ShapeDtypeStruct((rows, dtw), F32)],
        compiler_params=_cparams(2),
        name="inproj",
    )(u, w_in, w_in)


def _mlp_up_kernel(a_ref, w_ref, o_ref):
    acc = _dot(a_ref[...], w_ref[...].astype(BF16))
    o_ref[...] = jnp.square(jnp.maximum(acc, 0.0)).astype(o_ref.dtype)


def _mlp_up_call(h, w1, layer, *, rows):
    d = h.shape[1]
    dff = w1.shape[2]
    tm = _pick_tile(rows, 2112)
    tn = 512
    return pl.pallas_call(
        _mlp_up_kernel,
        grid=(rows // tm, dff // tn),
        in_specs=[pl.BlockSpec((tm, d), lambda i, j: (i, 0)),
                  pl.BlockSpec((None, d, tn), lambda i, j: (layer, 0, j))],
        out_specs=pl.BlockSpec((tm, tn), lambda i, j: (i, j)),
        out_shape=jax.ShapeDtypeStruct((rows, dff), BF16),
        compiler_params=_cparams(2),
        name="mlp_up",
    )(h, w1)


def _mm_res_kernel(*refs, tm, nj, n_lat, emit_next):
    if emit_next:
        (a_ref, w_ref, x_ref, g_ref, gt_ref, g2_ref, sh_ref, sc_ref, o_ref, u_ref, acc_ref) = refs
    else:
        (a_ref, w_ref, x_ref, g_ref, gt_ref, o_ref, acc_ref) = refs
    i = pl.program_id(0)
    j = pl.program_id(1)
    acc_ref[j] = _dot(a_ref[...], w_ref[...])
    rc = _row_chunk(tm)

    @pl.when(j == nj - 1)
    def _():
        def body(q, carry):
            r0 = pl.multiple_of(q * rc, rc)
            y = jnp.concatenate([acc_ref[jj, pl.ds(r0, rc), :] for jj in range(nj)], axis=1)
            ms = jnp.mean(y * y, axis=-1, keepdims=True)
            r = y * lax.rsqrt(ms + EPS) * g_ref[...]
            gt = _select_rows(gt_ref, i * tm + r0, rc, n_lat)
            xn = x_ref[pl.ds(r0, rc), :] + gt * r
            o_ref[pl.ds(r0, rc), :] = xn
            if emit_next:
                sh = _select_rows(sh_ref, i * tm + r0, rc, n_lat)
                sc = _select_rows(sc_ref, i * tm + r0, rc, n_lat)
                u_ref[pl.ds(r0, rc), :] = _norm_modulate(xn, g2_ref, sh, sc).astype(u_ref.dtype)
            return carry
        lax.fori_loop(0, tm // rc, body, 0)


def _mm_res_call(a, w_bf16, x, g, mod, layer, gt_idx, tn, *, rows, n_lat, name, nxt=None):
    kdim, d = w_bf16.shape
    tm = _pick_tile(rows, 528)
    nj = d // tn
    emit_next = nxt is not None
    kern = functools.partial(_mm_res_kernel, tm=tm, nj=nj, n_lat=n_lat, emit_next=emit_next)
    in_specs = [pl.BlockSpec((tm, kdim), lambda i, k: (i, 0)),
                pl.BlockSpec((kdim, tn), lambda i, k: (0, k)),
                pl.BlockSpec((tm, d), lambda i, k: (i, 0)),
                pl.BlockSpec((1, d), lambda i, k: (0, 0)),
                pl.BlockSpec((None, SUBLANE, d), lambda i, k: (layer, 0, gt_idx))]
    args = [a, w_bf16, x, g, mod]
    row = pl.BlockSpec((tm, d), lambda i, k: (i, 0))
    out_specs = row
    out_shape = jax.ShapeDtypeStruct((rows, d), F32)
    if emit_next:
        g2, layer2, sh_idx, sc_idx = nxt
        in_specs += [pl.BlockSpec((1, d), lambda i, k: (0, 0)),
                     pl.BlockSpec((None, SUBLANE, d), lambda i, k: (layer2, 0, sh_idx)),
                     pl.BlockSpec((None, SUBLANE, d), lambda i, k: (layer2, 0, sc_idx))]
        args += [g2, mod, mod]
        out_specs = [row, row]
        out_shape = [out_shape, jax.ShapeDtypeStruct((rows, d), BF16)]
    return pl.pallas_call(
        kern,
        grid=(rows // tm, nj),
        in_specs=in_specs,
        out_specs=out_specs,
        out_shape=out_shape,
        scratch_shapes=[pltpu.VMEM((nj, tm, tn), F32)],
        compiler_params=_cparams(2),
        name=name,
    )(*args)


def _conv_kernel(*refs, tt, n_lat, n_tot, rope):
    if rope:
        prev_ref, cur_ref, next_ref, w_ref, b_ref, cos_ref, sin_ref, o_ref, ext_ref = refs
    else:
        prev_ref, cur_ref, next_ref, w_ref, b_ref, o_ref, ext_ref = refs
    s0 = pl.program_id(0) * tt
    first = jnp.logical_or(s0 == 0, s0 == n_lat)
    last = jnp.logical_or(s0 + tt == n_lat, s0 + tt == n_tot)
    ext_ref[0:SUBLANE, :] = jnp.where(first, 0.0, prev_ref[...])
    ext_ref[SUBLANE:SUBLANE + tt, :] = cur_ref[...]
    ext_ref[SUBLANE + tt:2 * SUBLANE + tt, :] = jnp.where(last, 0.0, next_ref[...])
    tc = cur_ref.shape[1]
    rb, cbw = 64, 512
    half = SSD_CONV // 2
    for r0 in range(0, tt, rb):
        for c0 in range(0, tc, cbw):
            acc = b_ref[:, c0:c0 + cbw] + w_ref[0:1, c0:c0 + cbw] * ext_ref[
                r0 + SUBLANE - half:r0 + SUBLANE - half + rb, c0:c0 + cbw]
            for k in range(1, SSD_CONV):
                lo = r0 + SUBLANE - half + k
                acc = acc + w_ref[k:k + 1, c0:c0 + cbw] * ext_ref[lo:lo + rb, c0:c0 + cbw]
            y = acc * _sigmoid(acc)
            if rope:
                cosv = cos_ref[r0:r0 + rb, :]
                sinv = sin_ref[r0:r0 + rb, :]
                lane = lax.broadcasted_iota(jnp.int32, (rb, LANE), 1)
                low_half = (lane % 64) < 32
                for s in range(cbw // LANE):
                    yb = y[:, s * LANE:(s + 1) * LANE]
                    sw = jnp.where(low_half, pltpu.roll(yb, LANE - 32, 1), pltpu.roll(yb, 32, 1))
                    o_ref[r0:r0 + rb, c0 + s * LANE:c0 + (s + 1) * LANE] = (
                        yb * cosv + sw * sinv).astype(o_ref.dtype)
            else:
                o_ref[r0:r0 + rb, c0:c0 + cbw] = y.astype(o_ref.dtype)


def _conv_call(xbc, conv_w, conv_b, layer, col0, ncols, cos_t, sin_t, *, n_lat, rope, name):
    n_tot = xbc.shape[0]
    tt, tc = 256, 2048
    assert n_lat % tt == 0 and n_tot % tt == 0 and col0 % tc == 0 and ncols % tc == 0
    cb = col0 // tc
    hb = tt // SUBLANE
    nhb = n_tot // SUBLANE
    kern = functools.partial(_conv_kernel, tt=tt, n_lat=n_lat, n_tot=n_tot, rope=rope)
    in_specs = [pl.BlockSpec((SUBLANE, tc), lambda i, j: (jnp.maximum(i * hb - 1, 0), cb + j)),
                pl.BlockSpec((tt, tc), lambda i, j: (i, cb + j)),
                pl.BlockSpec((SUBLANE, tc), lambda i, j: (jnp.minimum((i + 1) * hb, nhb - 1), cb + j)),
                pl.BlockSpec((None, SSD_CONV, tc), lambda i, j: (layer, 0, cb + j)),
                pl.BlockSpec((None, 1, tc), lambda i, j: (layer, 0, cb + j))]
    args = [xbc, xbc, xbc, conv_w, conv_b.reshape(conv_b.shape[0], 1, -1)]
    if rope:
        in_specs += [pl.BlockSpec((tt, LANE), lambda i, j: (i, 0)),
                     pl.BlockSpec((tt, LANE), lambda i, j: (i, 0))]
        args += [cos_t, sin_t]
    return pl.pallas_call(
        kern,
        grid=(n_tot // tt, ncols // tc),
        in_specs=in_specs,
        out_specs=pl.BlockSpec((tt, tc), lambda i, j: (i, j)),
        out_shape=jax.ShapeDtypeStruct((n_tot, ncols), BF16),
        scratch_shapes=[pltpu.VMEM((tt + 2 * SUBLANE, tc), F32)],
        compiler_params=_cparams(2),
        name=name,
    )(*args)


def _dt_kernel(raw_ref, bias_ref, alog_ref, cs_ref, w_ref, ecs_ref, cst_ref, dtt_ref):
    raw = raw_ref[...] + bias_ref[...]
    dt = jnp.maximum(raw, 0.0) + jnp.log1p(jnp.exp(-jnp.abs(raw)))
    da_t = (dt * (-jnp.exp(alog_ref[...]))).T
    lane = lax.broadcasted_iota(jnp.int32, (LANE, SSD_CHUNK), 1)
    pre = da_t
    suf = da_t
    k = 1
    while k < SSD_CHUNK:
        pre = pre + jnp.where(lane >= k, pltpu.roll(pre, k, 1), 0.0)
        suf = suf + jnp.where(lane < SSD_CHUNK - k, pltpu.roll(suf, SSD_CHUNK - k, 1), 0.0)
        k *= 2
    row = lax.broadcasted_iota(jnp.int32, (LANE, SSD_CHUNK), 0)
    cst = jnp.where(row < SSD_HEADS, pre, suf)
    cs = cst.T
    col = lax.broadcasted_iota(jnp.int32, (1, LANE), 1)
    end = jnp.where(col < SSD_HEADS, cs[SSD_CHUNK - 1:SSD_CHUNK, :], cs[0:1, :])
    cs_ref[...] = cs * LOG2E
    w_ref[...] = jnp.exp(end - cs) * dt
    ecs_ref[...] = jnp.exp(cs)
    cst_ref[...] = cst * LOG2E
    dtt_ref[...] = dt.T


def _dt_call(dt_raw, dt_bias, a_log, layer):
    n_tot = dt_raw.shape[0]
    nch = n_tot // SSD_CHUNK
    row_major = pl.BlockSpec((SSD_CHUNK, LANE), lambda c: (c, 0))
    head_major = pl.BlockSpec((LANE, SSD_CHUNK), lambda c: (0, c))
    par = pl.BlockSpec((None, 1, LANE), lambda c: (layer, 0, 0))
    return pl.pallas_call(
        _dt_kernel,
        grid=(nch,),
        in_specs=[row_major, par, par],
        out_specs=[row_major, row_major, row_major, head_major, head_major],
        out_shape=[jax.ShapeDtypeStruct((n_tot, LANE), F32)] * 3
        + [jax.ShapeDtypeStruct((LANE, n_tot), F32)] * 2,
        compiler_params=_cparams(1),
        name="ssd_dt",
    )(dt_raw, dt_bias.reshape(-1, 1, LANE), a_log.reshape(-1, 1, LANE))


def _split_cat(v, parts):
    out = []
    rem = v
    for _ in range(parts):
        p = rem.astype(BF16)
        out.append(p)
        rem = rem - p.astype(F32)
    return jnp.concatenate(out, axis=1)


def _ssd_kernel(*refs, bwd):
    if bwd:
        (x_ref, bc_ref, cst_ref, dtt_ref, cs_ref, w_ref, ecs_ref, sel_ref,
         yf_ref, z_ref, dsk_ref, nw_ref, o_ref, h_ref) = refs
    else:
        (x_ref, bc_ref, cst_ref, dtt_ref, cs_ref, w_ref, ecs_ref, sel_ref, o_ref, h_ref) = refs

    @pl.when(pl.program_id(0) == 0)
    def _():
        h_ref[...] = jnp.zeros_like(h_ref)

    w_sp = _split_cat(w_ref[...], 2)
    e_sp = _split_cat(ecs_ref[...], 2)
    li = lax.broadcasted_iota(jnp.int32, (SSD_CHUNK, SSD_CHUNK), 0)
    si = lax.broadcasted_iota(jnp.int32, (SSD_CHUNK, SSD_CHUNK), 1)
    mask = (li <= si) if bwd else (li >= si)
    lane = lax.broadcasted_iota(jnp.int32, (SSD_CHUNK, LANE), 1)
    exit_row = 0 if bwd else SSD_CHUNK - 1
    dir_col = SSD_HEADS if bwd else 0

    for gi in range(SSD_GROUPS):
        gsl = slice(gi * GROUP_W, (gi + 1) * GROUP_W)
        x = x_ref[:, gsl]
        bm = bc_ref[:, gi * SSD_STATE:(gi + 1) * SSD_STATE]
        cm = bc_ref[:, (SSD_GROUPS + gi) * SSD_STATE:(SSD_GROUPS + gi + 1) * SSD_STATE]
        sel = sel_ref[gi]
        w_x = _dot(w_sp, jnp.concatenate([sel, sel], axis=0))
        e_x = _dot(e_sp, jnp.concatenate([sel, sel], axis=0))
        col0 = dir_col + gi * SSD_HPG
        cs8 = cs_ref[:, col0:col0 + SSD_HPG]

        cb = _dot_nt(cm, bm)
        h = h_ref[gi]
        y_off = _dot(cm, h.astype(BF16)) * e_x
        xw = (x.astype(F32) * w_x).astype(BF16)
        bt = bm.astype(F32).T.astype(BF16)
        h_ref[gi] = h * e_x[exit_row:exit_row + 1, :] + _dot(bt, xw)

        cst = cst_ref[gi * SSD_HPG:(gi + 1) * SSD_HPG, :]
        dtt = dtt_ref[gi * SSD_HPG:(gi + 1) * SSD_HPG, :]
        ys = []
        for j in range(SSD_HPG // 2):
            ms = []
            for e in (2 * j, 2 * j + 1):
                diff = cs8[:, e:e + 1] - cst[e:e + 1, :]
                m = jnp.where(mask, jnp.exp2(diff), 0.0) * (cb * dtt[e:e + 1, :])
                ms.append(m.astype(BF16))
            xp = x[:, j * LANE:(j + 1) * LANE]
            zero = jnp.zeros_like(xp)
            x2 = jnp.concatenate([jnp.where(lane < SSD_HEADDIM, xp, zero),
                                  jnp.where(lane >= SSD_HEADDIM, xp, zero)], axis=0)
            ys.append(_dot(jnp.concatenate(ms, axis=1), x2))
        y = y_off + jnp.concatenate(ys, axis=1)

        if bwd:
            yt = y + yf_ref[:, gsl] + dsk_ref[:, gsl] * x.astype(F32)
            zz = z_ref[:, gsl].astype(F32)
            g = yt * (zz * _sigmoid(zz))
            gs = g * lax.rsqrt(jnp.mean(g * g, axis=-1, keepdims=True) + EPS)
            o_ref[:, gsl] = (gs * nw_ref[:, gsl]).astype(o_ref.dtype)
        else:
            o_ref[:, gsl] = y


def _ssd_call(xs, bc, dtp, sel, *, n_lat, bwd, yf=None, z=None, dsk=None, nw=None):
    cs, w, ecs, cst, dtt = dtp
    n_tot = xs.shape[0]
    ncl = n_lat // SSD_CHUNK
    nch = n_tot // SSD_CHUNK
    ncc = nch - ncl
    d = 1 if bwd else 0
    if bwd:
        def cid(s):
            return nch - 1 - s
    else:
        def cid(s):
            return jnp.where(s < ncc, ncl + s, s - ncc)
    tile = pl.BlockSpec((SSD_CHUNK, D_INNER), lambda s: (cid(s), 0))
    full = pl.BlockSpec((SSD_CHUNK, LANE), lambda s: (cid(s), 0))
    heads = pl.BlockSpec((SSD_HEADS, SSD_CHUNK), lambda s: (d, cid(s)))
    in_specs = [tile,
                pl.BlockSpec((SSD_CHUNK, BC_DIM), lambda s: (cid(s), 0)),
                heads, heads, full, full, full,
                pl.BlockSpec((None, SSD_GROUPS, LANE, GROUP_W), lambda s: (d, 0, 0, 0))]
    args = [xs, bc, cst, dtt, cs, w, ecs, sel]
    if bwd:
        in_specs += [tile, tile,
                     pl.BlockSpec((1, D_INNER), lambda s: (0, 0)),
                     pl.BlockSpec((1, D_INNER), lambda s: (0, 0))]
        args += [yf, z, dsk, nw]
    return pl.pallas_call(
        functools.partial(_ssd_kernel, bwd=bwd),
        grid=(nch,),
        in_specs=in_specs,
        out_specs=tile,
        out_shape=jax.ShapeDtypeStruct((n_tot, D_INNER), BF16 if bwd else F32),
        scratch_shapes=[pltpu.VMEM((SSD_GROUPS, SSD_STATE, GROUP_W), F32)],
        compiler_params=_cparams(1),
        name="ssd_bwd" if bwd else "ssd_fwd",
    )(*args)


NA_QROWS = 4
NA_KROWS = NA_QROWS + NA_WIN_H
NA_TOK_BLK = 128


def _na_kernel(q_ref, k_ref, v_ref, bias_ref, o_ref, qt_ref, vt_ref, *, n_lat, n_ctx):
    rows = n_lat // GRID_W
    qscale = NA_HEAD_DIM ** -0.5 * LOG2E
    nq = NA_QROWS * GRID_W
    nkeys = NA_KROWS * GRID_W
    qblk = nq // NA_TOK_BLK
    kblk = nkeys // NA_TOK_BLK
    lat_blk = n_lat // NA_TOK_BLK
    ctx_blk = n_ctx // NA_TOK_BLK

    def transpose_blocks(p, carry):
        t0 = pl.multiple_of(p * NA_TOK_BLK, NA_TOK_BLK)
        qt_ref[p] = (q_ref[pl.ds(t0, NA_TOK_BLK), :].astype(F32) * qscale).T.astype(BF16)
        vt_ref[p] = v_ref[pl.ds(t0, NA_TOK_BLK), :].astype(F32).T.astype(BF16)
        return carry

    lax.fori_loop(0, lat_blk + ctx_blk, transpose_blocks, 0, unroll=2)

    kc = k_ref[n_lat:n_lat + n_ctx, :]
    vtc = jnp.concatenate([vt_ref[lat_blk + b] for b in range(ctx_blk)], axis=1)

    def softmax_pv(parts):
        m = parts[0][0].max(axis=0, keepdims=True)
        for s, _ in parts[1:]:
            m = jnp.maximum(m, s.max(axis=0, keepdims=True))
        den = 0.0
        acc = 0.0
        for s, vt in parts:
            p = jnp.exp2(s - m)
            den = den + p.sum(axis=0, keepdims=True)
            acc = acc + _dot(vt, p.astype(BF16))
        return acc / den

    def body(t, carry):
        r0 = t * NA_QROWS
        ws = jnp.clip(r0 - NA_WIN_H // 2, 0, rows - NA_WIN_H)
        variant = jnp.where(r0 == 0, 0, jnp.where(r0 == rows - NA_QROWS, 2, 1))
        q0 = pl.multiple_of(r0 * GRID_W, nq)
        k0 = pl.multiple_of(ws * GRID_W, nq)
        kb0 = ws * GRID_W // NA_TOK_BLK
        qt = jnp.concatenate([qt_ref[t * qblk + b] for b in range(qblk)], axis=1)
        vt = jnp.concatenate([vt_ref[kb0 + b] for b in range(kblk)], axis=1)
        s_loc = _dot(k_ref[pl.ds(k0, nkeys), :], qt) + bias_ref[variant]
        s_ctx = _dot(kc, qt)
        ot = softmax_pv([(s_loc, vt), (s_ctx, vtc)])
        o_ref[pl.ds(q0, nq), :] = ot.T.astype(o_ref.dtype)
        return carry

    lax.fori_loop(0, rows // NA_QROWS, body, 0, unroll=4)
    qtc = jnp.concatenate([qt_ref[lat_blk + b] for b in range(ctx_blk)], axis=1)
    otc = softmax_pv([(_dot(kc, qtc), vtc)])
    o_ref[n_lat:n_lat + n_ctx, :] = otc.T.astype(o_ref.dtype)


def _na_call(qkvg, bias, *, n_lat):
    n_tot = qkvg.shape[0]
    n_ctx = n_tot - n_lat
    nblk = n_tot // NA_TOK_BLK
    assert n_ctx >= NA_QROWS * GRID_W

    def col(base):
        return pl.BlockSpec((n_tot, NA_HEAD_DIM), lambda h: (0, base + h))
    return pl.pallas_call(
        functools.partial(_na_kernel, n_lat=n_lat, n_ctx=n_ctx),
        grid=(NA_HEADS,),
        in_specs=[col(0), col(NA_HEADS), col(2 * NA_HEADS),
                  pl.BlockSpec((None, 3, NA_KROWS * GRID_W, NA_QROWS * GRID_W), lambda h: (h, 0, 0, 0))],
        out_specs=col(0),
        out_shape=jax.ShapeDtypeStruct((n_tot, D_MODEL), BF16),
        scratch_shapes=[pltpu.VMEM((nblk, NA_HEAD_DIM, NA_TOK_BLK), BF16),
                        pltpu.VMEM((nblk, NA_HEAD_DIM, NA_TOK_BLK), BF16)],
        compiler_params=_cparams(1),
        name="na_attn",
    )(qkvg, qkvg, qkvg, bias)


def _na_bias_table(rpb):
    nh = rpb.shape[0]
    c = np.arange(GRID_W)[:, None]
    kc = np.arange(GRID_W)[None, :]
    cstart = np.clip(c - NA_WIN_W // 2, 0, GRID_W - NA_WIN_W)
    col_ok = (kc >= cstart) & (kc < cstart + NA_WIN_W)
    dcol = np.clip(kc - c, -(NA_WIN_W - 1), NA_WIN_W - 1) + NA_WIN_W - 1
    onehot = (dcol.T[None] == np.arange(2 * NA_WIN_W - 1)[:, None, None]).astype(np.float32)
    a = jnp.einsum("hdj,jkc->hdkc", rpb, jnp.asarray(onehot), precision=lax.Precision.HIGHEST)
    a = jnp.where(col_ok.T[None, None], a * LOG2E, -jnp.inf)
    neg = functools.partial(jnp.pad, mode="constant", constant_values=-jnp.inf)
    extra = NA_KROWS - NA_WIN_H
    groups = []
    for variant in range(3):
        per_row = []
        for i in range(NA_QROWS):
            if variant == 0:
                d0, before = NA_WIN_H - 1 - i, 0
            elif variant == 1:
                d0, before = NA_WIN_H // 2 - 1, i
            else:
                d0, before = NA_WIN_H // 2 - 1 - i, 0
            piece = a[:, d0:d0 + NA_WIN_H]
            per_row.append(neg(piece, ((0, 0), (before, extra - before), (0, 0), (0, 0))))
        groups.append(jnp.stack(per_row, axis=3))
    t = jnp.stack(groups, axis=1)
    return t.reshape(nh, 3, NA_KROWS * GRID_W, NA_QROWS * GRID_W).astype(F32)


def _gate_kernel(gn_ref, o_ref_in, ga_ref, gb_ref, w1_ref, w2_ref, out_ref):
    y1 = _dot(gn_ref[...], w1_ref[...].astype(BF16))
    y2 = _dot(o_ref_in[...], w2_ref[...].astype(BF16))
    out = _sigmoid(ga_ref[...].astype(F32)) * y1 + _sigmoid(gb_ref[...].astype(F32)) * y2
    out_ref[...] = out.astype(out_ref.dtype)


def _gate_call(gn, o_na, qkvg, w_ssd_o, w_na_o, layer, *, rows):
    tm = _pick_tile(rows, 1408)
    tn = 256
    ga0 = 3 * D_MODEL // tn
    gb0 = 4 * D_MODEL // tn
    return pl.pallas_call(
        _gate_kernel,
        grid=(rows // tm, D_MODEL // tn),
        in_specs=[pl.BlockSpec((tm, D_INNER), lambda i, j: (i, 0)),
                  pl.BlockSpec((tm, D_MODEL), lambda i, j: (i, 0)),
                  pl.BlockSpec((tm, tn), lambda i, j: (i, ga0 + j)),
                  pl.BlockSpec((tm, tn), lambda i, j: (i, gb0 + j)),
                  pl.BlockSpec((None, D_INNER, tn), lambda i, j: (layer, 0, j)),
                  pl.BlockSpec((None, D_MODEL, tn), lambda i, j: (layer, 0, j))],
        out_specs=pl.BlockSpec((tm, tn), lambda i, j: (i, j)),
        out_shape=jax.ShapeDtypeStruct((rows, D_MODEL), BF16),
        compiler_params=_cparams(2),
        name="gate_merge",
    )(gn, o_na, qkvg, qkvg, w_ssd_o, w_na_o)


def _rope_tables(n_lat, n_ctx):
    t = jnp.arange(n_lat, dtype=jnp.int32)
    n_ax = SSD_STATE // 2
    inv = ROPE_BASE ** (-jnp.arange(0, n_ax, 2, dtype=F32) / n_ax)
    ang_r = (t // GRID_W).astype(F32)[:, None] * inv
    ang_c = (t % GRID_W).astype(F32)[:, None] * inv
    cos_l = jnp.concatenate([jnp.cos(ang_r)] * 2 + [jnp.cos(ang_c)] * 2, axis=1)
    sin_l = jnp.concatenate([-jnp.sin(ang_r), jnp.sin(ang_r), -jnp.sin(ang_c), jnp.sin(ang_c)], axis=1)
    cos_t = jnp.concatenate([cos_l, jnp.ones((n_ctx, LANE), F32)], axis=0)
    sin_t = jnp.concatenate([sin_l, jnp.zeros((n_ctx, LANE), F32)], axis=0)
    return cos_t, sin_t


def _head_selectors():
    sel = np.zeros((2, SSD_GROUPS, LANE, GROUP_W), np.float32)
    for d in range(2):
        for g in range(SSD_GROUPS):
            for e in range(SSD_HPG):
                hd = d * SSD_HEADS + g * SSD_HPG + e
                sel[d, g, hd, e * SSD_HEADDIM:(e + 1) * SSD_HEADDIM] = 1.0
    return jnp.asarray(sel, BF16)


def kernel(x, c, ctx, c_ctx, w_ada, b_ada, g_pre_mix, g_post_mix, g_pre_mlp, g_post_mlp, w_in,
           conv_w, conv_b, a_log, dt_bias, d_skip, ssd_norm, w_ssd_o, rpb, w_na_o, w_out,
           w_mlp1, w_mlp2):
    assert x.shape[0] == 1 and x.shape[2] == D_MODEL
    depth = w_ada.shape[0]
    n_lat = x.shape[1]
    n_ctx = ctx.shape[1]
    n_tot = n_lat + n_ctx
    assert n_lat % (GRID_W * NA_WIN_H) == 0 and n_ctx % 256 == 0 and n_lat % 256 == 0

    cc = jnp.zeros((SUBLANE, D_MODEL), F32).at[0].set(c[0]).at[1].set(c_ctx)
    mod = _ada_call(cc, w_ada, b_ada)
    cos_t, sin_t = _rope_tables(n_lat, n_ctx)
    sel = _head_selectors()
    w_out_b = w_out.astype(BF16)
    w_mlp2_b = w_mlp2.astype(BF16)
    dsk_all = jnp.repeat(d_skip[:, 0] + d_skip[:, 1], SSD_HEADDIM, axis=-1)

    def row(v):
        return v.reshape(1, -1)

    stream, u = _stream_call(x[0], ctx[0], row(g_pre_mix[0]), mod)
    for l in range(depth):
        last = l == depth - 1
        rows_out = n_lat if last else n_tot
        z, xbc, qkvg, dt_raw = _inproj_call(u, w_in, l)

        xs = _conv_call(xbc, conv_w, conv_b, l, 0, D_INNER, None, None, n_lat=n_lat, rope=False,
                        name="conv_x")
        bc = _conv_call(xbc, conv_w, conv_b, l, D_INNER, BC_DIM, cos_t, sin_t, n_lat=n_lat,
                        rope=True, name="conv_bc")
        dtp = _dt_call(dt_raw, dt_bias, a_log, l)
        y_f = _ssd_call(xs, bc, dtp, sel, n_lat=n_lat, bwd=False)
        gn = _ssd_call(xs, bc, dtp, sel, n_lat=n_lat, bwd=True, yf=y_f, z=z,
                       dsk=row(dsk_all[l]), nw=row(ssd_norm[l]))

        o_na = _na_call(qkvg, _na_bias_table(rpb[l]), n_lat=n_lat)
        merged = _gate_call(gn, o_na, qkvg, w_ssd_o, w_na_o, l, rows=rows_out)
        stream_mid, h = _mm_res_call(merged, w_out_b[l], stream, row(g_post_mix[l]), mod, l, 2,
                                     1024, rows=rows_out, n_lat=n_lat, name="out_proj",
                                     nxt=(row(g_pre_mlp[l]), l, 3, 4))
        hidden = _mlp_up_call(h, w_mlp1, l, rows=rows_out)
        nxt = None if last else (row(g_pre_mix[l + 1]), l + 1, 0, 1)
        res = _mm_res_call(hidden, w_mlp2_b[l], stream_mid, row(g_post_mlp[l]), mod, l, 5, 256,
                           rows=rows_out, n_lat=n_lat, name="mlp_down", nxt=nxt)
        stream, u = (res, None) if last else res
    return stream[None]
```

```python
import functools

import numpy as np
import jax
import jax.numpy as jnp
from jax import lax
from jax.experimental import pallas as pl
from jax.experimental.pallas import tpu as pltpu

F32 = jnp.float32
BF16 = jnp.bfloat16

EPS = 1e-6
LOG2E = 1.4426950408889634
ROPE_BASE = 10000.0
GRID_W = 64
D_MODEL = 2048
D_INNER = 2 * D_MODEL
SSD_HEADDIM = 64
SSD_HEADS = D_INNER // SSD_HEADDIM
SSD_GROUPS = 8
SSD_HPG = SSD_HEADS // SSD_GROUPS
SSD_STATE = 128
SSD_CONV = 5
SSD_CHUNK = 128
GROUP_W = SSD_HPG * SSD_HEADDIM
BC_DIM = 2 * SSD_GROUPS * SSD_STATE
CONV_DIM = D_INNER + BC_DIM
NA_HEAD_DIM = 128
NA_HEADS = D_MODEL // NA_HEAD_DIM
NA_WIN_H = 8
NA_WIN_W = 16
D_FF = 4 * D_MODEL
OFF_Z = 0
OFF_XBC = D_INNER
OFF_DT = OFF_XBC + CONV_DIM
OFF_QKVG = OFF_DT + 2 * SSD_HEADS
QKVG_DIM = 5 * D_MODEL
PROJ_Z = 0
PROJ_XBC = D_INNER
PROJ_QKVG = PROJ_XBC + CONV_DIM
PROJ_DIM = PROJ_QKVG + QKVG_DIM

LANE = 128
SUBLANE = 8
BF16_ROWS = 16
VMEM_LIMIT_MB = 56


def _cparams(n_axes, vmem_mb=VMEM_LIMIT_MB):
    return pltpu.CompilerParams(dimension_semantics=("arbitrary",) * n_axes,
                                vmem_limit_bytes=vmem_mb * 1024 * 1024)


def _pick_tile(rows, cap):
    best = None
    for d in range(BF16_ROWS, cap + 1, BF16_ROWS):
        if rows % d == 0:
            best = d
    assert best is not None, (rows, cap)
    return best


EPILOGUE_ROWS = BF16_ROWS


def _sigmoid(v):
    return 1.0 / (1.0 + jnp.exp(-v))


def _dot(a, b):
    return jnp.dot(a, b, preferred_element_type=F32)


def _dot_nt(a, b):
    return lax.dot_general(a, b, (((1,), (1,)), ((), ())), preferred_element_type=F32)


def _select_rows(ref, row0, rc, n_lat):
    rows = row0 + lax.broadcasted_iota(jnp.int32, (rc, 1), 0)
    return jnp.where(rows >= n_lat, ref[1:2, :], ref[0:1, :])


def _ada_kernel(c_ref, w_ref, b_ref, o_ref):
    cv = c_ref[...]
    s = cv * _sigmoid(cv)
    o_ref[...] = _dot(s.astype(BF16), w_ref[...].astype(BF16)) + b_ref[...]


def _ada_call(cc, w_ada, b_ada):
    depth, d, n6 = w_ada.shape
    tn = 1024
    return pl.pallas_call(
        _ada_kernel,
        grid=(depth, n6 // tn),
        in_specs=[pl.BlockSpec((SUBLANE, d), lambda l, j: (0, 0)),
                  pl.BlockSpec((None, d, tn), lambda l, j: (l, 0, j)),
                  pl.BlockSpec((None, 1, tn), lambda l, j: (l, 0, j))],
        out_specs=pl.BlockSpec((None, SUBLANE, tn), lambda l, j: (l, 0, j)),
        out_shape=jax.ShapeDtypeStruct((depth, SUBLANE, n6), F32),
        compiler_params=_cparams(2),
        name="ada",
    )(cc, w_ada, b_ada.reshape(depth, 1, n6))


def _norm_modulate(v, g_ref, sh, sc):
    ms = jnp.mean(v * v, axis=-1, keepdims=True)
    return v * lax.rsqrt(ms + EPS) * g_ref[...] * (1.0 + sc) + sh


def _stream_kernel(x_ref, c_ref, g_ref, sh_ref, sc_ref, s_ref, u_ref, *, lat_tiles):
    i = pl.program_id(0)

    def emit(v, row):
        s_ref[...] = v
        u_ref[...] = _norm_modulate(v, g_ref, sh_ref[row:row + 1, :], sc_ref[row:row + 1, :]
                                    ).astype(u_ref.dtype)

    @pl.when(i < lat_tiles)
    def _():
        emit(x_ref[...], 0)

    @pl.when(i >= lat_tiles)
    def _():
        emit(c_ref[...], 1)


def _stream_call(x2, ctx2, g, mod):
    n_lat, d = x2.shape
    n_ctx = ctx2.shape[0]
    tr = 256
    lat_tiles = n_lat // tr
    row = pl.BlockSpec((tr, d), lambda i: (i, 0))
    return pl.pallas_call(
        functools.partial(_stream_kernel, lat_tiles=lat_tiles),
        grid=((n_lat + n_ctx) // tr,),
        in_specs=[pl.BlockSpec((tr, d), lambda i: (jnp.minimum(i, lat_tiles - 1), 0)),
                  pl.BlockSpec((tr, d), lambda i: (jnp.maximum(i - lat_tiles, 0), 0)),
                  pl.BlockSpec((1, d), lambda i: (0, 0)),
                  pl.BlockSpec((None, SUBLANE, d), lambda i: (0, 0, 0)),
                  pl.BlockSpec((None, SUBLANE, d), lambda i: (0, 0, 1))],
        out_specs=[row, row],
        out_shape=[jax.ShapeDtypeStruct((n_lat + n_ctx, d), F32),
                   jax.ShapeDtypeStruct((n_lat + n_ctx, d), BF16)],
        compiler_params=_cparams(1),
        name="stream",
    )(x2, ctx2, g, mod, mod)


def _inproj_kernel(a_ref, w_ref, wdt_ref, p_ref, dt_ref):
    a = a_ref[...]
    p_ref[...] = _dot(a, w_ref[0].astype(BF16)).astype(p_ref.dtype)

    @pl.when(pl.program_id(1) == 0)
    def _():
        dt_ref[...] = _dot(a, wdt_ref[...].astype(BF16))


def _inproj_call(u, w_in, layer):
    rows, d = u.shape
    tm = _pick_tile(rows, 2112)
    tn = 512
    dtw = 2 * SSD_HEADS

    def wcol(i, j):
        return (layer, 0, pl.multiple_of(j * tn + jnp.where(j >= PROJ_QKVG // tn, dtw, 0), dtw))
    return pl.pallas_call(
        _inproj_kernel,
        grid=(rows // tm, PROJ_DIM // tn),
        in_specs=[pl.BlockSpec((tm, d), lambda i, j: (i, 0)),
                  pl.BlockSpec((pl.Element(1), pl.Element(d), pl.Element(tn)), wcol),
                  pl.BlockSpec((None, d, dtw), lambda i, j: (layer, 0, OFF_DT // dtw))],
        out_specs=[pl.BlockSpec((tm, tn), lambda i, j: (i, j)),
                   pl.BlockSpec((tm, dtw), lambda i, j: (i, 0))],
        out_shape=[jax.ShapeDtypeStruct((rows, PROJ_DIM), BF16),
                   jax.ShapeDtypeStruct((rows, dtw), F32)],
        compiler_params=_cparams(2),
        name="inproj",
    )(u, w_in, w_in)


def _mlp_up_kernel(a_ref, w_ref, o_ref):
    acc = _dot(a_ref[...], w_ref[...].astype(BF16))
    o_ref[...] = jnp.square(jnp.maximum(acc, 0.0)).astype(o_ref.dtype)


def _mlp_up_call(h, w1, layer, *, rows):
    d = h.shape[1]
    dff = w1.shape[2]
    tm = _pick_tile(rows, 2112)
    tn = 512
    return pl.pallas_call(
        _mlp_up_kernel,
        grid=(rows // tm, dff // tn),
        in_specs=[pl.BlockSpec((tm, d), lambda i, j: (i, 0)),
                  pl.BlockSpec((None, d, tn), lambda i, j: (layer, 0, j))],
        out_specs=pl.BlockSpec((tm, tn), lambda i, j: (i, j)),
        out_shape=jax.ShapeDtypeStruct((rows, dff), BF16),
        compiler_params=_cparams(2),
        name="mlp_up",
    )(h, w1)


def _mm_res_kernel(*refs, tm, nj, n_lat, emit_next, all_latent):
    if emit_next:
        (a_ref, w_ref, x_ref, g_ref, gt_ref, g2_ref, sh_ref, sc_ref, o_ref, u_ref, acc_ref) = refs
    else:
        (a_ref, w_ref, x_ref, g_ref, gt_ref, o_ref, acc_ref) = refs
    i = pl.program_id(0)
    j = pl.program_id(1)
    acc_ref[j] = _dot(a_ref[...], w_ref[...])
    rc = EPILOGUE_ROWS
    tile_lo = i * tm

    def epilogue(pick):
        def body(q, carry):
            r0 = pl.multiple_of(q * rc, rc)
            y = jnp.concatenate([acc_ref[jj, pl.ds(r0, rc), :] for jj in range(nj)], axis=1)
            ms = jnp.mean(y * y, axis=-1, keepdims=True)
            r = y * lax.rsqrt(ms + EPS) * g_ref[...]
            xn = x_ref[pl.ds(r0, rc), :] + pick(gt_ref, r0) * r
            o_ref[pl.ds(r0, rc), :] = xn
            if emit_next:
                u_ref[pl.ds(r0, rc), :] = _norm_modulate(
                    xn, g2_ref, pick(sh_ref, r0), pick(sc_ref, r0)).astype(u_ref.dtype)
            return carry
        lax.fori_loop(0, tm // rc, body, 0, unroll=3 if (tm // rc) % 3 == 0 else 2)

    last = j == nj - 1
    if all_latent:
        pl.when(last)(lambda: epilogue(lambda ref, r0: ref[0:1, :]))
    else:
        side = (tile_lo >= n_lat).astype(jnp.int32)
        mixed = jnp.logical_and(tile_lo < n_lat, tile_lo + tm > n_lat)
        pl.when(jnp.logical_and(last, jnp.logical_not(mixed)))(
            lambda: epilogue(lambda ref, r0: ref[pl.ds(side, 1), :]))
        pl.when(jnp.logical_and(last, mixed))(
            lambda: epilogue(lambda ref, r0: _select_rows(ref, tile_lo + r0, rc, n_lat)))


def _mm_res_call(a, w_bf16, x, g, mod, layer, gt_idx, tn, *, rows, n_lat, name, nxt=None):
    kdim, d = w_bf16.shape
    tm = _pick_tile(rows, 528)
    nj = d // tn
    emit_next = nxt is not None
    kern = functools.partial(_mm_res_kernel, tm=tm, nj=nj, n_lat=n_lat, emit_next=emit_next,
                             all_latent=rows <= n_lat)
    in_specs = [pl.BlockSpec((tm, kdim), lambda i, k: (i, 0)),
                pl.BlockSpec((kdim, tn), lambda i, k: (0, k)),
                pl.BlockSpec((tm, d), lambda i, k: (i, 0)),
                pl.BlockSpec((1, d), lambda i, k: (0, 0)),
                pl.BlockSpec((None, SUBLANE, d), lambda i, k: (layer, 0, gt_idx))]
    args = [a, w_bf16, x, g, mod]
    row = pl.BlockSpec((tm, d), lambda i, k: (i, 0))
    out_specs = row
    out_shape = jax.ShapeDtypeStruct((rows, d), F32)
    if emit_next:
        g2, layer2, sh_idx, sc_idx = nxt
        in_specs += [pl.BlockSpec((1, d), lambda i, k: (0, 0)),
                     pl.BlockSpec((None, SUBLANE, d), lambda i, k: (layer2, 0, sh_idx)),
                     pl.BlockSpec((None, SUBLANE, d), lambda i, k: (layer2, 0, sc_idx))]
        args += [g2, mod, mod]
        out_specs = [row, row]
        out_shape = [out_shape, jax.ShapeDtypeStruct((rows, d), BF16)]
    return pl.pallas_call(
        kern,
        grid=(rows // tm, nj),
        in_specs=in_specs,
        out_specs=out_specs,
        out_shape=out_shape,
        scratch_shapes=[pltpu.VMEM((nj, tm, tn), F32)],
        compiler_params=_cparams(2),
        name=name,
    )(*args)


CONV_RB = 128

def _conv_kernel(*refs, tt, n_lat, n_tot, rope):
    if rope:
        prev_ref, cur_ref, next_ref, sm_ref, w_ref, b_ref, cos_ref, sin_ref, o_ref = refs
    else:
        prev_ref, cur_ref, next_ref, sm_ref, w_ref, b_ref, o_ref = refs
    s0 = pl.program_id(0) * tt
    first = jnp.logical_or(s0 == 0, s0 == n_lat)
    last = jnp.logical_or(s0 + tt == n_lat, s0 + tt == n_tot)
    zero = jnp.zeros_like(prev_ref[...])
    xe = jnp.concatenate([jnp.where(first, zero, prev_ref[...]), cur_ref[...],
                          jnp.where(last, zero, next_ref[...])], axis=0)
    tc = cur_ref.shape[1]
    rb, cbw = CONV_RB, 512
    half = SSD_CONV // 2
    taps = [k for k in range(SSD_CONV) if k != half]
    for r0 in range(0, tt, rb):
        sh = _dot(sm_ref[...], xe[r0:r0 + rb + 2 * BF16_ROWS, :])
        for c0 in range(0, tc, cbw):
            acc = b_ref[:, c0:c0 + cbw] + w_ref[half:half + 1, c0:c0 + cbw] * cur_ref[
                r0:r0 + rb, c0:c0 + cbw].astype(F32)
            for t, k in enumerate(taps):
                acc = acc + w_ref[k:k + 1, c0:c0 + cbw] * sh[t * rb:(t + 1) * rb, c0:c0 + cbw]
            y = acc * _sigmoid(acc)
            if rope:
                cosv = cos_ref[r0:r0 + rb, :]
                sinv = sin_ref[r0:r0 + rb, :]
                lane = lax.broadcasted_iota(jnp.int32, (rb, LANE), 1)
                low_half = (lane % 64) < 32
                for s in range(cbw // LANE):
                    yb = y[:, s * LANE:(s + 1) * LANE]
                    sw = jnp.where(low_half, pltpu.roll(yb, LANE - 32, 1), pltpu.roll(yb, 32, 1))
                    o_ref[r0:r0 + rb, c0 + s * LANE:c0 + (s + 1) * LANE] = (
                        yb * cosv + sw * sinv).astype(o_ref.dtype)
            else:
                o_ref[r0:r0 + rb, c0:c0 + cbw] = y.astype(o_ref.dtype)


def _conv_call(proj, shift, conv_w, conv_b, layer, col0, ncols, cos_t, sin_t, *, n_lat, rope, name):
    n_tot = proj.shape[0]
    tt, tc = 256, 2048
    assert n_lat % tt == 0 and n_tot % tt == 0 and col0 % tc == 0 and ncols % tc == 0
    assert PROJ_XBC % tc == 0
    cb = col0 // tc
    pb = (PROJ_XBC + col0) // tc
    hb = tt // BF16_ROWS
    nhb = n_tot // BF16_ROWS
    kern = functools.partial(_conv_kernel, tt=tt, n_lat=n_lat, n_tot=n_tot, rope=rope)
    in_specs = [pl.BlockSpec((BF16_ROWS, tc), lambda i, j: (jnp.maximum(i * hb - 1, 0), pb + j)),
                pl.BlockSpec((tt, tc), lambda i, j: (i, pb + j)),
                pl.BlockSpec((BF16_ROWS, tc), lambda i, j: (jnp.minimum((i + 1) * hb, nhb - 1), pb + j)),
                pl.BlockSpec(shift.shape, lambda i, j: (0, 0)),
                pl.BlockSpec((None, SSD_CONV, tc), lambda i, j: (layer, 0, cb + j)),
                pl.BlockSpec((None, 1, tc), lambda i, j: (layer, 0, cb + j))]
    args = [proj, proj, proj, shift, conv_w, conv_b.reshape(conv_b.shape[0], 1, -1)]
    if rope:
        in_specs += [pl.BlockSpec((tt, LANE), lambda i, j: (i, 0)),
                     pl.BlockSpec((tt, LANE), lambda i, j: (i, 0))]
        args += [cos_t, sin_t]
    return pl.pallas_call(
        kern,
        grid=(n_tot // tt, ncols // tc),
        in_specs=in_specs,
        out_specs=pl.BlockSpec((tt, tc), lambda i, j: (i, j)),
        out_shape=jax.ShapeDtypeStruct((n_tot, ncols), BF16),
        compiler_params=_cparams(2),
        name=name,
    )(*args)


def _conv_shift_matrix():
    half = SSD_CONV // 2
    offs = [k - half for k in range(SSD_CONV) if k != half]
    m = np.zeros((len(offs) * CONV_RB, CONV_RB + 2 * BF16_ROWS), np.float32)
    for t, o in enumerate(offs):
        m[t * CONV_RB + np.arange(CONV_RB), BF16_ROWS + np.arange(CONV_RB) + o] = 1.0
    return jnp.asarray(m, BF16)


def _dt_kernel(raw_ref, bias_ref, alog_ref, cs_ref, w_ref, ecs_ref, cst_ref, dtt_ref):
    raw = raw_ref[...] + bias_ref[...]
    dt = jnp.maximum(raw, 0.0) + jnp.log1p(jnp.exp(-jnp.abs(raw)))
    da_t = (dt * (-jnp.exp(alog_ref[...]))).T
    lane = lax.broadcasted_iota(jnp.int32, (LANE, SSD_CHUNK), 1)
    pre = da_t
    suf = da_t
    k = 1
    while k < SSD_CHUNK:
        pre = pre + jnp.where(lane >= k, pltpu.roll(pre, k, 1), 0.0)
        suf = suf + jnp.where(lane < SSD_CHUNK - k, pltpu.roll(suf, SSD_CHUNK - k, 1), 0.0)
        k *= 2
    row = lax.broadcasted_iota(jnp.int32, (LANE, SSD_CHUNK), 0)
    cst = jnp.where(row < SSD_HEADS, pre, suf)
    cs = cst.T
    col = lax.broadcasted_iota(jnp.int32, (1, LANE), 1)
    end = jnp.where(col < SSD_HEADS, cs[SSD_CHUNK - 1:SSD_CHUNK, :], cs[0:1, :])
    cs_ref[...] = cs * LOG2E
    w_ref[...] = jnp.exp(end - cs) * dt
    ecs_ref[...] = jnp.exp(cs)
    cst_ref[...] = cst * LOG2E
    dtt_ref[...] = dt.T


def _dt_call(dt_raw, dt_bias, a_log, layer):
    n_tot = dt_raw.shape[0]
    nch = n_tot // SSD_CHUNK
    row_major = pl.BlockSpec((SSD_CHUNK, LANE), lambda c: (c, 0))
    head_major = pl.BlockSpec((LANE, SSD_CHUNK), lambda c: (0, c))
    par = pl.BlockSpec((None, 1, LANE), lambda c: (layer, 0, 0))
    return pl.pallas_call(
        _dt_kernel,
        grid=(nch,),
        in_specs=[row_major, par, par],
        out_specs=[row_major, row_major, row_major, head_major, head_major],
        out_shape=[jax.ShapeDtypeStruct((n_tot, LANE), F32)] * 3
        + [jax.ShapeDtypeStruct((LANE, n_tot), F32)] * 2,
        compiler_params=_cparams(1),
        name="ssd_dt",
    )(dt_raw, dt_bias.reshape(-1, 1, LANE), a_log.reshape(-1, 1, LANE))


def _split_cat(v, parts):
    out = []
    rem = v
    for _ in range(parts):
        p = rem.astype(BF16)
        out.append(p)
        rem = rem - p.astype(F32)
    return jnp.concatenate(out, axis=1)


def _ssd_kernel(*refs, bwd):
    if bwd:
        (x_ref, bc_ref, cst_ref, dtt_ref, cs_ref, w_ref, ecs_ref, sel_ref,
         yf_ref, z_ref, dsk_ref, nw_ref, o_ref, h_ref) = refs
    else:
        (x_ref, bc_ref, cst_ref, dtt_ref, cs_ref, w_ref, ecs_ref, sel_ref, o_ref, h_ref) = refs

    @pl.when(pl.program_id(0) == 0)
    def _():
        h_ref[...] = jnp.zeros_like(h_ref)

    w_sp = _split_cat(w_ref[...], 2)
    e_sp = _split_cat(ecs_ref[...], 2)
    li = lax.broadcasted_iota(jnp.int32, (SSD_CHUNK, SSD_CHUNK), 0)
    si = lax.broadcasted_iota(jnp.int32, (SSD_CHUNK, SSD_CHUNK), 1)
    mask = (li <= si) if bwd else (li >= si)
    lane = lax.broadcasted_iota(jnp.int32, (SSD_CHUNK, LANE), 1)
    exit_row = 0 if bwd else SSD_CHUNK - 1
    dir_col = SSD_HEADS if bwd else 0

    for gi in range(SSD_GROUPS):
        gsl = slice(gi * GROUP_W, (gi + 1) * GROUP_W)
        x = x_ref[:, gsl]
        bm = bc_ref[:, gi * SSD_STATE:(gi + 1) * SSD_STATE]
        cm = bc_ref[:, (SSD_GROUPS + gi) * SSD_STATE:(SSD_GROUPS + gi + 1) * SSD_STATE]
        sel = sel_ref[gi]
        w_x = _dot(w_sp, jnp.concatenate([sel, sel], axis=0))
        e_x = _dot(e_sp, jnp.concatenate([sel, sel], axis=0))
        col0 = dir_col + gi * SSD_HPG
        cs8 = cs_ref[:, col0:col0 + SSD_HPG]

        cb = _dot_nt(cm, bm)
        h = h_ref[gi]
        y_off = _dot(cm, h.astype(BF16)) * e_x
        xw = (x.astype(F32) * w_x).astype(BF16)
        bt = bm.astype(F32).T.astype(BF16)
        h_ref[gi] = h * e_x[exit_row:exit_row + 1, :] + _dot(bt, xw)

        cst = cst_ref[gi * SSD_HPG:(gi + 1) * SSD_HPG, :]
        dtt = dtt_ref[gi * SSD_HPG:(gi + 1) * SSD_HPG, :]
        ys = []
        for j in range(SSD_HPG // 2):
            ms = []
            for e in (2 * j, 2 * j + 1):
                diff = cs8[:, e:e + 1] - cst[e:e + 1, :]
                m = jnp.where(mask, jnp.exp2(diff), 0.0) * (cb * dtt[e:e + 1, :])
                ms.append(m.astype(BF16))
            xp = x[:, j * LANE:(j + 1) * LANE]
            zero = jnp.zeros_like(xp)
            x2 = jnp.concatenate([jnp.where(lane < SSD_HEADDIM, xp, zero),
                                  jnp.where(lane >= SSD_HEADDIM, xp, zero)], axis=0)
            ys.append(_dot(jnp.concatenate(ms, axis=1), x2))
        y = y_off + jnp.concatenate(ys, axis=1)

        if bwd:
            yt = y + yf_ref[:, gsl] + dsk_ref[:, gsl] * x.astype(F32)
            zz = z_ref[:, gsl].astype(F32)
            g = yt * (zz * _sigmoid(zz))
            gs = g * lax.rsqrt(jnp.mean(g * g, axis=-1, keepdims=True) + EPS)
            o_ref[:, gsl] = (gs * nw_ref[:, gsl]).astype(o_ref.dtype)
        else:
            o_ref[:, gsl] = y


def _ssd_call(xs, bc, dtp, sel, *, n_lat, bwd, yf=None, z=None, dsk=None, nw=None):
    cs, w, ecs, cst, dtt = dtp
    n_tot = xs.shape[0]
    ncl = n_lat // SSD_CHUNK
    nch = n_tot // SSD_CHUNK
    ncc = nch - ncl
    d = 1 if bwd else 0
    if bwd:
        def cid(s):
            return nch - 1 - s
    else:
        def cid(s):
            return jnp.where(s < ncc, ncl + s, s - ncc)
    tile = pl.BlockSpec((SSD_CHUNK, D_INNER), lambda s: (cid(s), 0))
    full = pl.BlockSpec((SSD_CHUNK, LANE), lambda s: (cid(s), 0))
    heads = pl.BlockSpec((SSD_HEADS, SSD_CHUNK), lambda s: (d, cid(s)))
    in_specs = [tile,
                pl.BlockSpec((SSD_CHUNK, BC_DIM), lambda s: (cid(s), 0)),
                heads, heads, full, full, full,
                pl.BlockSpec((None, SSD_GROUPS, LANE, GROUP_W), lambda s: (d, 0, 0, 0))]
    args = [xs, bc, cst, dtt, cs, w, ecs, sel]
    if bwd:
        in_specs += [tile, tile,
                     pl.BlockSpec((1, D_INNER), lambda s: (0, 0)),
                     pl.BlockSpec((1, D_INNER), lambda s: (0, 0))]
        args += [yf, z, dsk, nw]
    return pl.pallas_call(
        functools.partial(_ssd_kernel, bwd=bwd),
        grid=(nch,),
        in_specs=in_specs,
        out_specs=tile,
        out_shape=jax.ShapeDtypeStruct((n_tot, D_INNER), BF16 if bwd else F32),
        scratch_shapes=[pltpu.VMEM((SSD_GROUPS, SSD_STATE, GROUP_W), F32)],
        compiler_params=_cparams(1),
        name="ssd_bwd" if bwd else "ssd_fwd",
    )(*args)


NA_QROWS = 4
NA_KROWS = NA_QROWS + NA_WIN_H
NA_TOK_BLK = 128


def _na_fill_bias(pp_ref, bias_ref):
    lane = lax.broadcasted_iota(jnp.int32, (GRID_W, 2 * GRID_W), 1)
    ninf = jnp.full((GRID_W, 2 * GRID_W), -jnp.inf, F32)
    for v in range(3):
        for kk in range(NA_KROWS):
            for p in range(NA_QROWS // 2):
                i0 = 2 * p
                if v == 1:
                    ok0, ok1 = 0 <= kk - i0 < NA_WIN_H, 0 <= kk - i0 - 1 < NA_WIN_H
                    d0 = kk - i0 + NA_WIN_H // 2 - 1
                else:
                    ok0 = ok1 = kk < NA_WIN_H
                    d0 = kk - i0 + (NA_WIN_H - 1 if v == 0 else NA_WIN_H // 2 - 1)
                if ok0 and ok1:
                    blk = pp_ref[d0]
                elif ok0:
                    blk = jnp.where(lane < GRID_W, pp_ref[d0], ninf)
                elif ok1:
                    blk = jnp.where(lane >= GRID_W, pp_ref[d0], ninf)
                else:
                    blk = ninf
                bias_ref[v, kk * GRID_W:(kk + 1) * GRID_W, p * 2 * GRID_W:(p + 1) * 2 * GRID_W] = blk


def _na_kernel(q_ref, k_ref, v_ref, pp_ref, o_ref, qt_ref, vt_ref, bias_ref, *, n_lat, n_ctx):
    _na_fill_bias(pp_ref, bias_ref)
    rows = n_lat // GRID_W
    qscale = NA_HEAD_DIM ** -0.5 * LOG2E
    nq = NA_QROWS * GRID_W
    nkeys = NA_KROWS * GRID_W
    qblk = nq // NA_TOK_BLK
    kblk = nkeys // NA_TOK_BLK
    lat_blk = n_lat // NA_TOK_BLK
    ctx_blk = n_ctx // NA_TOK_BLK

    def transpose_blocks(p, carry):
        t0 = pl.multiple_of(p * NA_TOK_BLK, NA_TOK_BLK)
        qt_ref[p] = (q_ref[pl.ds(t0, NA_TOK_BLK), :].astype(F32) * qscale).T.astype(BF16)
        vt_ref[p] = v_ref[pl.ds(t0, NA_TOK_BLK), :].astype(F32).T.astype(BF16)
        return carry

    lax.fori_loop(0, lat_blk + ctx_blk, transpose_blocks, 0, unroll=2)

    kc = k_ref[n_lat:n_lat + n_ctx, :]
    vtc = jnp.concatenate([vt_ref[lat_blk + b] for b in range(ctx_blk)], axis=1)

    def softmax_pv(parts):
        m = parts[0][0].max(axis=0, keepdims=True)
        for s, _ in parts[1:]:
            m = jnp.maximum(m, s.max(axis=0, keepdims=True))
        den = 0.0
        acc = 0.0
        for s, vt in parts:
            p = jnp.exp2(s - m)
            den = den + p.sum(axis=0, keepdims=True)
            acc = acc + _dot(vt, p.astype(BF16))
        return acc / den

    def body(t, carry):
        r0 = t * NA_QROWS
        ws = jnp.clip(r0 - NA_WIN_H // 2, 0, rows - NA_WIN_H)
        variant = jnp.where(r0 == 0, 0, jnp.where(r0 == rows - NA_QROWS, 2, 1))
        q0 = pl.multiple_of(r0 * GRID_W, nq)
        k0 = pl.multiple_of(ws * GRID_W, nq)
        kb0 = ws * GRID_W // NA_TOK_BLK
        qt = jnp.concatenate([qt_ref[t * qblk + b] for b in range(qblk)], axis=1)
        vt = jnp.concatenate([vt_ref[kb0 + b] for b in range(kblk)], axis=1)
        s_loc = _dot(k_ref[pl.ds(k0, nkeys), :], qt) + bias_ref[variant]
        s_ctx = _dot(kc, qt)
        ot = softmax_pv([(s_loc, vt), (s_ctx, vtc)])
        o_ref[pl.ds(q0, nq), :] = ot.T.astype(o_ref.dtype)
        return carry

    lax.fori_loop(0, rows // NA_QROWS, body, 0, unroll=4)
    qtc = jnp.concatenate([qt_ref[lat_blk + b] for b in range(ctx_blk)], axis=1)
    otc = softmax_pv([(_dot(kc, qtc), vtc)])
    o_ref[n_lat:n_lat + n_ctx, :] = otc.T.astype(o_ref.dtype)


def _na_call(proj, pairs, *, n_lat):
    n_tot = proj.shape[0]
    n_ctx = n_tot - n_lat
    nblk = n_tot // NA_TOK_BLK
    assert n_ctx >= NA_QROWS * GRID_W
    q0 = PROJ_QKVG // NA_HEAD_DIM

    def col(base):
        return pl.BlockSpec((n_tot, NA_HEAD_DIM), lambda h: (0, base + h))
    return pl.pallas_call(
        functools.partial(_na_kernel, n_lat=n_lat, n_ctx=n_ctx),
        grid=(NA_HEADS,),
        in_specs=[col(q0), col(q0 + NA_HEADS), col(q0 + 2 * NA_HEADS),
                  pl.BlockSpec((None, 2 * NA_WIN_H - 1, GRID_W, 2 * GRID_W), lambda h: (h, 0, 0, 0))],
        out_specs=col(0),
        out_shape=jax.ShapeDtypeStruct((n_tot, D_MODEL), BF16),
        scratch_shapes=[pltpu.VMEM((nblk, NA_HEAD_DIM, NA_TOK_BLK), BF16),
                        pltpu.VMEM((nblk, NA_HEAD_DIM, NA_TOK_BLK), BF16),
                        pltpu.VMEM((3, NA_KROWS * GRID_W, NA_QROWS * GRID_W), F32)],
        compiler_params=_cparams(1),
        name="na_attn",
    )(proj, proj, proj, pairs)


def _na_bias_pairs(rpb):
    c = np.arange(GRID_W)[:, None]
    kc = np.arange(GRID_W)[None, :]
    cstart = np.clip(c - NA_WIN_W // 2, 0, GRID_W - NA_WIN_W)
    col_ok = (kc >= cstart) & (kc < cstart + NA_WIN_W)
    dcol = np.clip(kc - c, -(NA_WIN_W - 1), NA_WIN_W - 1) + NA_WIN_W - 1
    onehot = (dcol.T[None] == np.arange(2 * NA_WIN_W - 1)[:, None, None]).astype(np.float32)
    a = jnp.einsum("hdj,jkc->hdkc", rpb, jnp.asarray(onehot), precision=lax.Precision.HIGHEST)
    a = jnp.where(col_ok.T[None, None], a * LOG2E, -jnp.inf).astype(F32)
    prev = jnp.concatenate([jnp.full_like(a[:, :1], -jnp.inf), a[:, :-1]], axis=1)
    return jnp.concatenate([a, prev], axis=-1)


def _gate_kernel(gn_ref, o_ref_in, ga_ref, gb_ref, w1_ref, w2_ref, out_ref):
    y1 = _dot(gn_ref[...], w1_ref[...].astype(BF16))
    y2 = _dot(o_ref_in[...], w2_ref[...].astype(BF16))
    out = _sigmoid(ga_ref[...].astype(F32)) * y1 + _sigmoid(gb_ref[...].astype(F32)) * y2
    out_ref[...] = out.astype(out_ref.dtype)


def _gate_call(gn, o_na, proj, w_ssd_o, w_na_o, layer, *, rows):
    tm = _pick_tile(rows, 1408)
    tn = 256
    ga0 = (PROJ_QKVG + 3 * D_MODEL) // tn
    gb0 = (PROJ_QKVG + 4 * D_MODEL) // tn
    return pl.pallas_call(
        _gate_kernel,
        grid=(rows // tm, D_MODEL // tn),
        in_specs=[pl.BlockSpec((tm, D_INNER), lambda i, j: (i, 0)),
                  pl.BlockSpec((tm, D_MODEL), lambda i, j: (i, 0)),
                  pl.BlockSpec((tm, tn), lambda i, j: (i, ga0 + j)),
                  pl.BlockSpec((tm, tn), lambda i, j: (i, gb0 + j)),
                  pl.BlockSpec((None, D_INNER, tn), lambda i, j: (layer, 0, j)),
                  pl.BlockSpec((None, D_MODEL, tn), lambda i, j: (layer, 0, j))],
        out_specs=pl.BlockSpec((tm, tn), lambda i, j: (i, j)),
        out_shape=jax.ShapeDtypeStruct((rows, D_MODEL), BF16),
        compiler_params=_cparams(2),
        name="gate_merge",
    )(gn, o_na, proj, proj, w_ssd_o, w_na_o)


def _rope_tables(n_lat, n_ctx):
    t = jnp.arange(n_lat, dtype=jnp.int32)
    n_ax = SSD_STATE // 2
    inv = ROPE_BASE ** (-jnp.arange(0, n_ax, 2, dtype=F32) / n_ax)
    ang_r = (t // GRID_W).astype(F32)[:, None] * inv
    ang_c = (t % GRID_W).astype(F32)[:, None] * inv
    cos_l = jnp.concatenate([jnp.cos(ang_r)] * 2 + [jnp.cos(ang_c)] * 2, axis=1)
    sin_l = jnp.concatenate([-jnp.sin(ang_r), jnp.sin(ang_r), -jnp.sin(ang_c), jnp.sin(ang_c)], axis=1)
    cos_t = jnp.concatenate([cos_l, jnp.ones((n_ctx, LANE), F32)], axis=0)
    sin_t = jnp.concatenate([sin_l, jnp.zeros((n_ctx, LANE), F32)], axis=0)
    return cos_t, sin_t


def _head_selectors():
    sel = np.zeros((2, SSD_GROUPS, LANE, GROUP_W), np.float32)
    for d in range(2):
        for g in range(SSD_GROUPS):
            for e in range(SSD_HPG):
                hd = d * SSD_HEADS + g * SSD_HPG + e
                sel[d, g, hd, e * SSD_HEADDIM:(e + 1) * SSD_HEADDIM] = 1.0
    return jnp.asarray(sel, BF16)


def kernel(x, c, ctx, c_ctx, w_ada, b_ada, g_pre_mix, g_post_mix, g_pre_mlp, g_post_mlp, w_in,
           conv_w, conv_b, a_log, dt_bias, d_skip, ssd_norm, w_ssd_o, rpb, w_na_o, w_out,
           w_mlp1, w_mlp2):
    assert x.shape[0] == 1 and x.shape[2] == D_MODEL
    depth = w_ada.shape[0]
    n_lat = x.shape[1]
    n_ctx = ctx.shape[1]
    n_tot = n_lat + n_ctx
    assert n_lat % (GRID_W * NA_WIN_H) == 0 and n_ctx % 256 == 0 and n_lat % 256 == 0

    cc = jnp.zeros((SUBLANE, D_MODEL), F32).at[0].set(c[0]).at[1].set(c_ctx)
    mod = _ada_call(cc, w_ada, b_ada)
    cos_t, sin_t = _rope_tables(n_lat, n_ctx)
    sel = _head_selectors()
    shift = _conv_shift_matrix()
    w_out_b = w_out.astype(BF16)
    w_mlp2_b = w_mlp2.astype(BF16)
    dsk_all = jnp.repeat(d_skip[:, 0] + d_skip[:, 1], SSD_HEADDIM, axis=-1)

    def row(v):
        return v.reshape(1, -1)

    stream, u = _stream_call(x.reshape(n_lat, D_MODEL), ctx.reshape(n_ctx, D_MODEL),
                             row(g_pre_mix[0]), mod)
    for l in range(depth):
        last = l == depth - 1
        rows_out = n_lat if last else n_tot
        proj, dt_raw = _inproj_call(u, w_in, l)

        xs = _conv_call(proj, shift, conv_w, conv_b, l, 0, D_INNER, None, None, n_lat=n_lat,
                        rope=False, name="conv_x")
        bc = _conv_call(proj, shift, conv_w, conv_b, l, D_INNER, BC_DIM, cos_t, sin_t, n_lat=n_lat,
                        rope=True, name="conv_bc")
        dtp = _dt_call(dt_raw, dt_bias, a_log, l)
        y_f = _ssd_call(xs, bc, dtp, sel, n_lat=n_lat, bwd=False)
        gn = _ssd_call(xs, bc, dtp, sel, n_lat=n_lat, bwd=True, yf=y_f, z=proj,
                       dsk=row(dsk_all[l]), nw=row(ssd_norm[l]))

        o_na = _na_call(proj, _na_bias_pairs(rpb[l]), n_lat=n_lat)
        merged = _gate_call(gn, o_na, proj, w_ssd_o, w_na_o, l, rows=rows_out)
        stream_mid, h = _mm_res_call(merged, w_out_b[l], stream, row(g_post_mix[l]), mod, l, 2,
                                     1024, rows=rows_out, n_lat=n_lat, name="out_proj",
                                     nxt=(row(g_pre_mlp[l]), l, 3, 4))
        hidden = _mlp_up_call(h, w_mlp1, l, rows=rows_out)
        nxt = None if last else (row(g_pre_mix[l + 1]), l + 1, 0, 1)
        res = _mm_res_call(hidden, w_mlp2_b[l], stream_mid, row(g_post_mlp[l]), mod, l, 5, 256,
                           rows=rows_out, n_lat=n_lat, name="mlp_down", nxt=nxt)
        stream, u = (res, None) if last else res
    return stream[None]
```

```python
import functools

import numpy as np
import jax
import jax.numpy as jnp
from jax import lax
from jax.experimental import pallas as pl
from jax.experimental.pallas import tpu as pltpu

F32 = jnp.float32
BF16 = jnp.bfloat16

EPS = 1e-6
LOG2E = 1.4426950408889634
ROPE_BASE = 10000.0
GRID_W = 64
D_MODEL = 2048
D_INNER = 2 * D_MODEL
SSD_HEADDIM = 64
SSD_HEADS = D_INNER // SSD_HEADDIM
SSD_GROUPS = 8
SSD_HPG = SSD_HEADS // SSD_GROUPS
SSD_STATE = 128
SSD_CONV = 5
SSD_CHUNK = 128
GROUP_W = SSD_HPG * SSD_HEADDIM
BC_DIM = 2 * SSD_GROUPS * SSD_STATE
CONV_DIM = D_INNER + BC_DIM
NA_HEAD_DIM = 128
NA_HEADS = D_MODEL // NA_HEAD_DIM
NA_WIN_H = 8
NA_WIN_W = 16
D_FF = 4 * D_MODEL
OFF_Z = 0
OFF_XBC = D_INNER
OFF_DT = OFF_XBC + CONV_DIM
OFF_QKVG = OFF_DT + 2 * SSD_HEADS
QKVG_DIM = 5 * D_MODEL
PROJ_Z = 0
PROJ_XBC = D_INNER
PROJ_QKVG = PROJ_XBC + CONV_DIM
PROJ_DIM = PROJ_QKVG + QKVG_DIM

LANE = 128
SUBLANE = 8
BF16_ROWS = 16
VMEM_LIMIT_MB = 56


def _cparams(n_axes, vmem_mb=VMEM_LIMIT_MB):
    return pltpu.CompilerParams(dimension_semantics=("arbitrary",) * n_axes,
                                vmem_limit_bytes=vmem_mb * 1024 * 1024)


def _pick_tile(rows, cap):
    best = None
    for d in range(BF16_ROWS, cap + 1, BF16_ROWS):
        if rows % d == 0:
            best = d
    assert best is not None, (rows, cap)
    return best


EPILOGUE_ROWS = BF16_ROWS


def _sigmoid(v):
    return 1.0 / (1.0 + jnp.exp(-v))


def _dot(a, b):
    return jnp.dot(a, b, preferred_element_type=F32)


def _dot_nt(a, b):
    return lax.dot_general(a, b, (((1,), (1,)), ((), ())), preferred_element_type=F32)


def _select_rows(ref, row0, rc, n_lat):
    rows = row0 + lax.broadcasted_iota(jnp.int32, (rc, 1), 0)
    return jnp.where(rows >= n_lat, ref[1:2, :], ref[0:1, :])


def _ada_kernel(c_ref, w_ref, b_ref, o_ref):
    cv = c_ref[...]
    s = cv * _sigmoid(cv)
    o_ref[...] = _dot(s.astype(BF16), w_ref[...].astype(BF16)) + b_ref[...]


def _ada_call(cc, w_ada, b_ada):
    depth, d, n6 = w_ada.shape
    tn = 1024
    return pl.pallas_call(
        _ada_kernel,
        grid=(depth, n6 // tn),
        in_specs=[pl.BlockSpec((SUBLANE, d), lambda l, j: (0, 0)),
                  pl.BlockSpec((None, d, tn), lambda l, j: (l, 0, j)),
                  pl.BlockSpec((None, 1, tn), lambda l, j: (l, 0, j))],
        out_specs=pl.BlockSpec((None, SUBLANE, tn), lambda l, j: (l, 0, j)),
        out_shape=jax.ShapeDtypeStruct((depth, SUBLANE, n6), F32),
        compiler_params=_cparams(2),
        name="ada",
    )(cc, w_ada, b_ada.reshape(depth, 1, n6))


def _norm_modulate(v, g_ref, sh, sc):
    ms = jnp.mean(v * v, axis=-1, keepdims=True)
    return v * lax.rsqrt(ms + EPS) * g_ref[...] * (1.0 + sc) + sh


def _stream_kernel(x_ref, c_ref, g_ref, sh_ref, sc_ref, s_ref, u_ref, *, lat_tiles):
    i = pl.program_id(0)

    def emit(v, row):
        s_ref[...] = v
        u_ref[...] = _norm_modulate(v, g_ref, sh_ref[row:row + 1, :], sc_ref[row:row + 1, :]
                                    ).astype(u_ref.dtype)

    @pl.when(i < lat_tiles)
    def _():
        emit(x_ref[...], 0)

    @pl.when(i >= lat_tiles)
    def _():
        emit(c_ref[...], 1)


def _stream_call(x2, ctx2, g, mod):
    n_lat, d = x2.shape
    n_ctx = ctx2.shape[0]
    tr = 256
    lat_tiles = n_lat // tr
    row = pl.BlockSpec((tr, d), lambda i: (i, 0))
    return pl.pallas_call(
        functools.partial(_stream_kernel, lat_tiles=lat_tiles),
        grid=((n_lat + n_ctx) // tr,),
        in_specs=[pl.BlockSpec((tr, d), lambda i: (jnp.minimum(i, lat_tiles - 1), 0)),
                  pl.BlockSpec((tr, d), lambda i: (jnp.maximum(i - lat_tiles, 0), 0)),
                  pl.BlockSpec((1, d), lambda i: (0, 0)),
                  pl.BlockSpec((None, SUBLANE, d), lambda i: (0, 0, 0)),
                  pl.BlockSpec((None, SUBLANE, d), lambda i: (0, 0, 1))],
        out_specs=[row, row],
        out_shape=[jax.ShapeDtypeStruct((n_lat + n_ctx, d), F32),
                   jax.ShapeDtypeStruct((n_lat + n_ctx, d), BF16)],
        compiler_params=_cparams(1),
        name="stream",
    )(x2, ctx2, g, mod, mod)


def _inproj_kernel(a_ref, w_ref, wdt_ref, p_ref, dt_ref):
    a = a_ref[...]
    p_ref[...] = _dot(a, w_ref[0].astype(BF16)).astype(p_ref.dtype)

    @pl.when(pl.program_id(1) == 0)
    def _():
        dt_ref[...] = _dot(a, wdt_ref[...].astype(BF16))


def _inproj_call(u, w_in, layer):
    rows, d = u.shape
    tm = _pick_tile(rows, 2112)
    tn = 512
    dtw = 2 * SSD_HEADS

    def wcol(i, j):
        return (layer, 0, pl.multiple_of(j * tn + jnp.where(j >= PROJ_QKVG // tn, dtw, 0), dtw))
    return pl.pallas_call(
        _inproj_kernel,
        grid=(rows // tm, PROJ_DIM // tn),
        in_specs=[pl.BlockSpec((tm, d), lambda i, j: (i, 0)),
                  pl.BlockSpec((pl.Element(1), pl.Element(d), pl.Element(tn)), wcol),
                  pl.BlockSpec((None, d, dtw), lambda i, j: (layer, 0, OFF_DT // dtw))],
        out_specs=[pl.BlockSpec((tm, tn), lambda i, j: (i, j)),
                   pl.BlockSpec((tm, dtw), lambda i, j: (i, 0))],
        out_shape=[jax.ShapeDtypeStruct((rows, PROJ_DIM), BF16),
                   jax.ShapeDtypeStruct((rows, dtw), F32)],
        compiler_params=_cparams(2),
        name="inproj",
    )(u, w_in, w_in)


def _mlp_up_kernel(a_ref, w_ref, o_ref):
    acc = _dot(a_ref[...], w_ref[...].astype(BF16))
    o_ref[...] = jnp.square(jnp.maximum(acc, 0.0)).astype(o_ref.dtype)


def _mlp_up_call(h, w1, layer, *, rows):
    d = h.shape[1]
    dff = w1.shape[2]
    tm = _pick_tile(rows, 2112)
    tn = 512
    return pl.pallas_call(
        _mlp_up_kernel,
        grid=(rows // tm, dff // tn),
        in_specs=[pl.BlockSpec((tm, d), lambda i, j: (i, 0)),
                  pl.BlockSpec((None, d, tn), lambda i, j: (layer, 0, j))],
        out_specs=pl.BlockSpec((tm, tn), lambda i, j: (i, j)),
        out_shape=jax.ShapeDtypeStruct((rows, dff), BF16),
        compiler_params=_cparams(2),
        name="mlp_up",
    )(h, w1)


def _mm_res_kernel(*refs, tm, nj, n_lat, emit_next, all_latent):
    if emit_next:
        (a_ref, w_ref, x_ref, g_ref, gt_ref, g2_ref, sh_ref, sc_ref, o_ref, u_ref, acc_ref) = refs
    else:
        (a_ref, w_ref, x_ref, g_ref, gt_ref, o_ref, acc_ref) = refs
    i = pl.program_id(0)
    j = pl.program_id(1)
    acc_ref[j] = _dot(a_ref[...], w_ref[...])
    rc = EPILOGUE_ROWS
    tile_lo = i * tm

    def epilogue(pick):
        def body(q, carry):
            r0 = pl.multiple_of(q * rc, rc)
            y = jnp.concatenate([acc_ref[jj, pl.ds(r0, rc), :] for jj in range(nj)], axis=1)
            ms = jnp.mean(y * y, axis=-1, keepdims=True)
            r = y * lax.rsqrt(ms + EPS) * g_ref[...]
            xn = x_ref[pl.ds(r0, rc), :] + pick(gt_ref, r0) * r
            o_ref[pl.ds(r0, rc), :] = xn
            if emit_next:
                u_ref[pl.ds(r0, rc), :] = _norm_modulate(
                    xn, g2_ref, pick(sh_ref, r0), pick(sc_ref, r0)).astype(u_ref.dtype)
            return carry
        lax.fori_loop(0, tm // rc, body, 0, unroll=3 if (tm // rc) % 3 == 0 else 2)

    last = j == nj - 1
    if all_latent:
        pl.when(last)(lambda: epilogue(lambda ref, r0: ref[0:1, :]))
    else:
        side = (tile_lo >= n_lat).astype(jnp.int32)
        mixed = jnp.logical_and(tile_lo < n_lat, tile_lo + tm > n_lat)
        pl.when(jnp.logical_and(last, jnp.logical_not(mixed)))(
            lambda: epilogue(lambda ref, r0: ref[pl.ds(side, 1), :]))
        pl.when(jnp.logical_and(last, mixed))(
            lambda: epilogue(lambda ref, r0: _select_rows(ref, tile_lo + r0, rc, n_lat)))


def _mm_res_call(a, w_bf16, x, g, mod, layer, gt_idx, tn, *, rows, n_lat, name, nxt=None):
    _, kdim, d = w_bf16.shape
    tm = _pick_tile(rows, 528)
    nj = d // tn
    emit_next = nxt is not None
    kern = functools.partial(_mm_res_kernel, tm=tm, nj=nj, n_lat=n_lat, emit_next=emit_next,
                             all_latent=rows <= n_lat)
    in_specs = [pl.BlockSpec((tm, kdim), lambda i, k: (i, 0)),
                pl.BlockSpec((None, kdim, tn), lambda i, k: (layer, 0, k)),
                pl.BlockSpec((tm, d), lambda i, k: (i, 0)),
                pl.BlockSpec((1, d), lambda i, k: (0, 0)),
                pl.BlockSpec((None, SUBLANE, d), lambda i, k: (layer, 0, gt_idx))]
    args = [a, w_bf16, x, g, mod]
    row = pl.BlockSpec((tm, d), lambda i, k: (i, 0))
    out_specs = row
    out_shape = jax.ShapeDtypeStruct((rows, d), F32)
    if emit_next:
        g2, layer2, sh_idx, sc_idx = nxt
        in_specs += [pl.BlockSpec((1, d), lambda i, k: (0, 0)),
                     pl.BlockSpec((None, SUBLANE, d), lambda i, k: (layer2, 0, sh_idx)),
                     pl.BlockSpec((None, SUBLANE, d), lambda i, k: (layer2, 0, sc_idx))]
        args += [g2, mod, mod]
        out_specs = [row, row]
        out_shape = [out_shape, jax.ShapeDtypeStruct((rows, d), BF16)]
    return pl.pallas_call(
        kern,
        grid=(rows // tm, nj),
        in_specs=in_specs,
        out_specs=out_specs,
        out_shape=out_shape,
        scratch_shapes=[pltpu.VMEM((nj, tm, tn), F32)],
        compiler_params=_cparams(2),
        name=name,
    )(*args)


CONV_RB = 128

def _conv_kernel(*refs, tt, n_lat, n_tot, rope):
    if rope:
        prev_ref, cur_ref, next_ref, sm_ref, w_ref, b_ref, cos_ref, sin_ref, o_ref = refs
    else:
        prev_ref, cur_ref, next_ref, sm_ref, w_ref, b_ref, o_ref = refs
    s0 = pl.program_id(0) * tt
    first = jnp.logical_or(s0 == 0, s0 == n_lat)
    last = jnp.logical_or(s0 + tt == n_lat, s0 + tt == n_tot)
    zero = jnp.zeros_like(prev_ref[...])
    xe = jnp.concatenate([jnp.where(first, zero, prev_ref[...]), cur_ref[...],
                          jnp.where(last, zero, next_ref[...])], axis=0)
    tc = cur_ref.shape[1]
    rb, cbw = CONV_RB, 512
    half = SSD_CONV // 2
    taps = [k for k in range(SSD_CONV) if k != half]
    for r0 in range(0, tt, rb):
        sh = _dot(sm_ref[...], xe[r0:r0 + rb + 2 * BF16_ROWS, :])
        for c0 in range(0, tc, cbw):
            acc = b_ref[:, c0:c0 + cbw] + w_ref[half:half + 1, c0:c0 + cbw] * cur_ref[
                r0:r0 + rb, c0:c0 + cbw].astype(F32)
            for t, k in enumerate(taps):
                acc = acc + w_ref[k:k + 1, c0:c0 + cbw] * sh[t * rb:(t + 1) * rb, c0:c0 + cbw]
            y = acc * _sigmoid(acc)
            if rope:
                cosv = cos_ref[r0:r0 + rb, :]
                sinv = sin_ref[r0:r0 + rb, :]
                lane = lax.broadcasted_iota(jnp.int32, (rb, LANE), 1)
                low_half = (lane % 64) < 32
                for s in range(cbw // LANE):
                    yb = y[:, s * LANE:(s + 1) * LANE]
                    sw = jnp.where(low_half, pltpu.roll(yb, LANE - 32, 1), pltpu.roll(yb, 32, 1))
                    o_ref[r0:r0 + rb, c0 + s * LANE:c0 + (s + 1) * LANE] = (
                        yb * cosv + sw * sinv).astype(o_ref.dtype)
            else:
                o_ref[r0:r0 + rb, c0:c0 + cbw] = y.astype(o_ref.dtype)


def _conv_call(proj, shift, conv_w, conv_b, layer, col0, ncols, cos_t, sin_t, *, n_lat, rope, name):
    n_tot = proj.shape[0]
    tt, tc = 256, 2048
    assert n_lat % tt == 0 and n_tot % tt == 0 and col0 % tc == 0 and ncols % tc == 0
    assert PROJ_XBC % tc == 0
    cb = col0 // tc
    pb = (PROJ_XBC + col0) // tc
    hb = tt // BF16_ROWS
    nhb = n_tot // BF16_ROWS
    kern = functools.partial(_conv_kernel, tt=tt, n_lat=n_lat, n_tot=n_tot, rope=rope)
    in_specs = [pl.BlockSpec((BF16_ROWS, tc), lambda i, j: (jnp.maximum(i * hb - 1, 0), pb + j)),
                pl.BlockSpec((tt, tc), lambda i, j: (i, pb + j)),
                pl.BlockSpec((BF16_ROWS, tc), lambda i, j: (jnp.minimum((i + 1) * hb, nhb - 1), pb + j)),
                pl.BlockSpec(shift.shape, lambda i, j: (0, 0)),
                pl.BlockSpec((None, SSD_CONV, tc), lambda i, j: (layer, 0, cb + j)),
                pl.BlockSpec((None, 1, tc), lambda i, j: (layer, 0, cb + j))]
    args = [proj, proj, proj, shift, conv_w, conv_b.reshape(conv_b.shape[0], 1, -1)]
    if rope:
        in_specs += [pl.BlockSpec((tt, LANE), lambda i, j: (i, 0)),
                     pl.BlockSpec((tt, LANE), lambda i, j: (i, 0))]
        args += [cos_t, sin_t]
    return pl.pallas_call(
        kern,
        grid=(n_tot // tt, ncols // tc),
        in_specs=in_specs,
        out_specs=pl.BlockSpec((tt, tc), lambda i, j: (i, j)),
        out_shape=jax.ShapeDtypeStruct((n_tot, ncols), BF16),
        compiler_params=_cparams(2),
        name=name,
    )(*args)


def _conv_shift_matrix():
    half = SSD_CONV // 2
    offs = [k - half for k in range(SSD_CONV) if k != half]
    m = np.zeros((len(offs) * CONV_RB, CONV_RB + 2 * BF16_ROWS), np.float32)
    for t, o in enumerate(offs):
        m[t * CONV_RB + np.arange(CONV_RB), BF16_ROWS + np.arange(CONV_RB) + o] = 1.0
    return jnp.asarray(m, BF16)


def _dt_kernel(raw_ref, bias_ref, alog_ref, cs_ref, w_ref, ecs_ref, cst_ref):
    lane = lax.broadcasted_iota(jnp.int32, (LANE, SSD_CHUNK), 1)
    row = lax.broadcasted_iota(jnp.int32, (LANE, SSD_CHUNK), 0)
    col = lax.broadcasted_iota(jnp.int32, (1, LANE), 1)
    for c in range(DT_CHUNKS):
        tok = slice(c * SSD_CHUNK, (c + 1) * SSD_CHUNK)
        raw = raw_ref[tok, :] + bias_ref[...]
        dt = jnp.maximum(raw, 0.0) + jnp.log1p(jnp.exp(-jnp.abs(raw)))
        da_t = (dt * (-jnp.exp(alog_ref[...]))).T
        pre = da_t
        suf = da_t
        k = 1
        while k < SSD_CHUNK:
            pre = pre + jnp.where(lane >= k, pltpu.roll(pre, k, 1), 0.0)
            suf = suf + jnp.where(lane < SSD_CHUNK - k, pltpu.roll(suf, SSD_CHUNK - k, 1), 0.0)
            k *= 2
        cst = jnp.where(row < SSD_HEADS, pre, suf)
        cs = cst.T
        end = jnp.where(col < SSD_HEADS, cs[SSD_CHUNK - 1:SSD_CHUNK, :], cs[0:1, :])
        cs_ref[tok, :] = cs * LOG2E
        w_ref[tok, :] = jnp.exp(end - cs) * dt
        ecs_ref[tok, :] = jnp.exp(cs)
        cst_ref[:, tok] = cst * LOG2E - jnp.log2(dt).T


DT_CHUNKS = 2


def _dt_call(dt_raw, dt_bias, a_log, layer):
    n_tot = dt_raw.shape[0]
    span = DT_CHUNKS * SSD_CHUNK
    assert n_tot % span == 0
    row_major = pl.BlockSpec((span, LANE), lambda c: (c, 0))
    head_major = pl.BlockSpec((LANE, span), lambda c: (0, c))
    par = pl.BlockSpec((None, 1, LANE), lambda c: (layer, 0, 0))
    return pl.pallas_call(
        _dt_kernel,
        grid=(n_tot // span,),
        in_specs=[row_major, par, par],
        out_specs=[row_major, row_major, row_major, head_major],
        out_shape=[jax.ShapeDtypeStruct((n_tot, LANE), F32)] * 3
        + [jax.ShapeDtypeStruct((LANE, n_tot), F32)],
        compiler_params=_cparams(1),
        name="ssd_dt",
    )(dt_raw, dt_bias.reshape(-1, 1, LANE), a_log.reshape(-1, 1, LANE))


def _split_cat(v, parts):
    out = []
    rem = v
    for _ in range(parts):
        p = rem.astype(BF16)
        out.append(p)
        rem = rem - p.astype(F32)
    return jnp.concatenate(out, axis=1)


def _ssd_kernel(*refs, bwd):
    if bwd:
        (x_ref, bc_ref, cst_ref, cs_ref, w_ref, ecs_ref, sel_ref,
         yf_ref, z_ref, dsk_ref, nw_ref, o_ref, h_ref) = refs
    else:
        (x_ref, bc_ref, cst_ref, cs_ref, w_ref, ecs_ref, sel_ref, o_ref, h_ref) = refs

    @pl.when(pl.program_id(0) == 0)
    def _():
        h_ref[...] = jnp.zeros_like(h_ref)

    w_sp = _split_cat(w_ref[...], 2)
    e_sp = _split_cat(ecs_ref[...], 2)
    li = lax.broadcasted_iota(jnp.int32, (SSD_CHUNK, SSD_CHUNK), 0)
    si = lax.broadcasted_iota(jnp.int32, (SSD_CHUNK, SSD_CHUNK), 1)
    mask = (li <= si) if bwd else (li >= si)
    lane = lax.broadcasted_iota(jnp.int32, (SSD_CHUNK, LANE), 1)
    exit_row = 0 if bwd else SSD_CHUNK - 1
    dir_col = SSD_HEADS if bwd else 0

    for gi in range(SSD_GROUPS):
        gsl = slice(gi * GROUP_W, (gi + 1) * GROUP_W)
        x = x_ref[:, gsl]
        bm = bc_ref[:, gi * SSD_STATE:(gi + 1) * SSD_STATE]
        cm = bc_ref[:, (SSD_GROUPS + gi) * SSD_STATE:(SSD_GROUPS + gi + 1) * SSD_STATE]
        sel = sel_ref[gi]
        w_x = _dot(w_sp, jnp.concatenate([sel, sel], axis=0))
        e_x = _dot(e_sp, jnp.concatenate([sel, sel], axis=0))
        col0 = dir_col + gi * SSD_HPG
        cs8 = cs_ref[:, col0:col0 + SSD_HPG]

        cb = _dot_nt(cm, bm)
        h = h_ref[gi]
        y_off = _dot(cm, h.astype(BF16)) * e_x
        xw = (x.astype(F32) * w_x).astype(BF16)
        bt = bm.astype(F32).T.astype(BF16)
        h_ref[gi] = h * e_x[exit_row:exit_row + 1, :] + _dot(bt, xw)

        cst = cst_ref[gi * SSD_HPG:(gi + 1) * SSD_HPG, :]
        ys = []
        for j in range(SSD_HPG // 2):
            ms = []
            for e in (2 * j, 2 * j + 1):
                diff = cs8[:, e:e + 1] - cst[e:e + 1, :]
                m = jnp.where(mask, jnp.exp2(diff), 0.0) * cb
                ms.append(m.astype(BF16))
            xp = x[:, j * LANE:(j + 1) * LANE]
            zero = jnp.zeros_like(xp)
            x2 = jnp.concatenate([jnp.where(lane < SSD_HEADDIM, xp, zero),
                                  jnp.where(lane >= SSD_HEADDIM, xp, zero)], axis=0)
            ys.append(_dot(jnp.concatenate(ms, axis=1), x2))
        y = y_off + jnp.concatenate(ys, axis=1)

        if bwd:
            yt = y + yf_ref[:, gsl] + dsk_ref[:, gsl] * x.astype(F32)
            zz = z_ref[:, gsl].astype(F32)
            g = yt * (zz * _sigmoid(zz))
            gs = g * lax.rsqrt(jnp.mean(g * g, axis=-1, keepdims=True) + EPS)
            o_ref[:, gsl] = (gs * nw_ref[:, gsl]).astype(o_ref.dtype)
        else:
            o_ref[:, gsl] = y


def _ssd_call(xs, bc, dtp, sel, *, n_lat, bwd, yf=None, z=None, dsk=None, nw=None):
    cs, w, ecs, cst = dtp
    n_tot = xs.shape[0]
    ncl = n_lat // SSD_CHUNK
    nch = n_tot // SSD_CHUNK
    ncc = nch - ncl
    d = 1 if bwd else 0
    if bwd:
        def cid(s):
            return nch - 1 - s
    else:
        def cid(s):
            return jnp.where(s < ncc, ncl + s, s - ncc)
    tile = pl.BlockSpec((SSD_CHUNK, D_INNER), lambda s: (cid(s), 0))
    full = pl.BlockSpec((SSD_CHUNK, LANE), lambda s: (cid(s), 0))
    heads = pl.BlockSpec((SSD_HEADS, SSD_CHUNK), lambda s: (d, cid(s)))
    in_specs = [tile,
                pl.BlockSpec((SSD_CHUNK, BC_DIM), lambda s: (cid(s), 0)),
                heads, full, full, full,
                pl.BlockSpec((None, SSD_GROUPS, LANE, GROUP_W), lambda s: (d, 0, 0, 0))]
    args = [xs, bc, cst, cs, w, ecs, sel]
    if bwd:
        in_specs += [tile, tile,
                     pl.BlockSpec((1, D_INNER), lambda s: (0, 0)),
                     pl.BlockSpec((1, D_INNER), lambda s: (0, 0))]
        args += [yf, z, dsk, nw]
    return pl.pallas_call(
        functools.partial(_ssd_kernel, bwd=bwd),
        grid=(nch,),
        in_specs=in_specs,
        out_specs=tile,
        out_shape=jax.ShapeDtypeStruct((n_tot, D_INNER), BF16 if bwd else F32),
        scratch_shapes=[pltpu.VMEM((SSD_GROUPS, SSD_STATE, GROUP_W), F32)],
        compiler_params=_cparams(1),
        name="ssd_bwd" if bwd else "ssd_fwd",
    )(*args)


NA_QROWS = 4
NA_KROWS = NA_QROWS + NA_WIN_H
NA_TOK_BLK = 128


def _na_fill_bias(pp_ref, bias_ref):
    lane = lax.broadcasted_iota(jnp.int32, (GRID_W, 2 * GRID_W), 1)
    ninf = jnp.full((GRID_W, 2 * GRID_W), -jnp.inf, F32)
    for v in range(3):
        for kk in range(NA_KROWS):
            for p in range(NA_QROWS // 2):
                i0 = 2 * p
                if v == 1:
                    ok0, ok1 = 0 <= kk - i0 < NA_WIN_H, 0 <= kk - i0 - 1 < NA_WIN_H
                    d0 = kk - i0 + NA_WIN_H // 2 - 1
                else:
                    ok0 = ok1 = kk < NA_WIN_H
                    d0 = kk - i0 + (NA_WIN_H - 1 if v == 0 else NA_WIN_H // 2 - 1)
                if ok0 and ok1:
                    blk = pp_ref[d0]
                elif ok0:
                    blk = jnp.where(lane < GRID_W, pp_ref[d0], ninf)
                elif ok1:
                    blk = jnp.where(lane >= GRID_W, pp_ref[d0], ninf)
                else:
                    blk = ninf
                bias_ref[v, kk * GRID_W:(kk + 1) * GRID_W, p * 2 * GRID_W:(p + 1) * 2 * GRID_W] = blk


def _na_kernel(q_ref, k_ref, v_ref, pp_ref, o_ref, qt_ref, vt_ref, bias_ref, *, n_lat, n_ctx):
    _na_fill_bias(pp_ref, bias_ref)
    rows = n_lat // GRID_W
    qscale = NA_HEAD_DIM ** -0.5 * LOG2E
    nq = NA_QROWS * GRID_W
    nkeys = NA_KROWS * GRID_W
    qblk = nq // NA_TOK_BLK
    kblk = nkeys // NA_TOK_BLK
    lat_blk = n_lat // NA_TOK_BLK
    ctx_blk = n_ctx // NA_TOK_BLK

    def transpose_blocks(p, carry):
        t0 = pl.multiple_of(p * NA_TOK_BLK, NA_TOK_BLK)
        qt_ref[p] = (q_ref[pl.ds(t0, NA_TOK_BLK), :].astype(F32) * qscale).astype(BF16).T
        vt_ref[p] = v_ref[pl.ds(t0, NA_TOK_BLK), :].T
        return carry

    lax.fori_loop(0, lat_blk + ctx_blk, transpose_blocks, 0, unroll=6)

    kc = k_ref[n_lat:n_lat + n_ctx, :]
    vtc = jnp.concatenate([vt_ref[lat_blk + b] for b in range(ctx_blk)], axis=1)

    def softmax_pv(parts):
        m = parts[0][0].max(axis=0, keepdims=True)
        for s, _ in parts[1:]:
            m = jnp.maximum(m, s.max(axis=0, keepdims=True))
        den = 0.0
        acc = 0.0
        for s, vt in parts:
            p = jnp.exp2(s - m)
            den = den + p.sum(axis=0, keepdims=True)
            acc = acc + _dot(vt, p.astype(BF16))
        return acc / den

    def body(t, carry):
        r0 = t * NA_QROWS
        ws = jnp.clip(r0 - NA_WIN_H // 2, 0, rows - NA_WIN_H)
        variant = jnp.where(r0 == 0, 0, jnp.where(r0 == rows - NA_QROWS, 2, 1))
        q0 = pl.multiple_of(r0 * GRID_W, nq)
        k0 = pl.multiple_of(ws * GRID_W, nq)
        kb0 = ws * GRID_W // NA_TOK_BLK
        qt = jnp.concatenate([qt_ref[t * qblk + b] for b in range(qblk)], axis=1)
        vt = jnp.concatenate([vt_ref[kb0 + b] for b in range(kblk)], axis=1)
        s_loc = _dot(k_ref[pl.ds(k0, nkeys), :], qt) + bias_ref[variant]
        s_ctx = _dot(kc, qt)
        ot = softmax_pv([(s_loc, vt), (s_ctx, vtc)])
        o_ref[pl.ds(q0, nq), :] = ot.T.astype(o_ref.dtype)
        return carry

    lax.fori_loop(0, rows // NA_QROWS, body, 0, unroll=4)
    qtc = jnp.concatenate([qt_ref[lat_blk + b] for b in range(ctx_blk)], axis=1)
    otc = softmax_pv([(_dot(kc, qtc), vtc)])
    o_ref[n_lat:n_lat + n_ctx, :] = otc.T.astype(o_ref.dtype)


def _na_call(proj, pairs, *, n_lat):
    n_tot = proj.shape[0]
    n_ctx = n_tot - n_lat
    nblk = n_tot // NA_TOK_BLK
    assert n_ctx >= NA_QROWS * GRID_W
    q0 = PROJ_QKVG // NA_HEAD_DIM

    def col(base):
        return pl.BlockSpec((n_tot, NA_HEAD_DIM), lambda h: (0, base + h))
    return pl.pallas_call(
        functools.partial(_na_kernel, n_lat=n_lat, n_ctx=n_ctx),
        grid=(NA_HEADS,),
        in_specs=[col(q0), col(q0 + NA_HEADS), col(q0 + 2 * NA_HEADS),
                  pl.BlockSpec((None, 2 * NA_WIN_H - 1, GRID_W, 2 * GRID_W), lambda h: (h, 0, 0, 0))],
        out_specs=col(0),
        out_shape=jax.ShapeDtypeStruct((n_tot, D_MODEL), BF16),
        scratch_shapes=[pltpu.VMEM((nblk, NA_HEAD_DIM, NA_TOK_BLK), BF16),
                        pltpu.VMEM((nblk, NA_HEAD_DIM, NA_TOK_BLK), BF16),
                        pltpu.VMEM((3, NA_KROWS * GRID_W, NA_QROWS * GRID_W), F32)],
        compiler_params=_cparams(1),
        name="na_attn",
    )(proj, proj, proj, pairs)


def _na_bias_pairs(rpb):
    c = np.arange(GRID_W)[:, None]
    kc = np.arange(GRID_W)[None, :]
    cstart = np.clip(c - NA_WIN_W // 2, 0, GRID_W - NA_WIN_W)
    col_ok = (kc >= cstart) & (kc < cstart + NA_WIN_W)
    dcol = np.clip(kc - c, -(NA_WIN_W - 1), NA_WIN_W - 1) + NA_WIN_W - 1
    onehot = (dcol.T[None] == np.arange(2 * NA_WIN_W - 1)[:, None, None]).astype(np.float32)
    a = jnp.einsum("hdj,jkc->hdkc", rpb, jnp.asarray(onehot), precision=lax.Precision.HIGHEST)
    a = jnp.where(col_ok.T[None, None], a * LOG2E, -jnp.inf).astype(F32)
    prev = jnp.concatenate([jnp.full_like(a[:, :1], -jnp.inf), a[:, :-1]], axis=1)
    return jnp.concatenate([a, prev], axis=-1)


def _gate_kernel(gn_ref, o_ref_in, ga_ref, gb_ref, w1_ref, w2_ref, out_ref):
    y1 = _dot(gn_ref[...], w1_ref[...].astype(BF16))
    y2 = _dot(o_ref_in[...], w2_ref[...].astype(BF16))
    out = _sigmoid(ga_ref[...].astype(F32)) * y1 + _sigmoid(gb_ref[...].astype(F32)) * y2
    out_ref[...] = out.astype(out_ref.dtype)


def _gate_call(gn, o_na, proj, w_ssd_o, w_na_o, layer, *, rows):
    tm = _pick_tile(rows, 1408)
    tn = 256
    ga0 = (PROJ_QKVG + 3 * D_MODEL) // tn
    gb0 = (PROJ_QKVG + 4 * D_MODEL) // tn
    return pl.pallas_call(
        _gate_kernel,
        grid=(rows // tm, D_MODEL // tn),
        in_specs=[pl.BlockSpec((tm, D_INNER), lambda i, j: (i, 0)),
                  pl.BlockSpec((tm, D_MODEL), lambda i, j: (i, 0)),
                  pl.BlockSpec((tm, tn), lambda i, j: (i, ga0 + j)),
                  pl.BlockSpec((tm, tn), lambda i, j: (i, gb0 + j)),
                  pl.BlockSpec((None, D_INNER, tn), lambda i, j: (layer, 0, j)),
                  pl.BlockSpec((None, D_MODEL, tn), lambda i, j: (layer, 0, j))],
        out_specs=pl.BlockSpec((tm, tn), lambda i, j: (i, j)),
        out_shape=jax.ShapeDtypeStruct((rows, D_MODEL), BF16),
        compiler_params=_cparams(2),
        name="gate_merge",
    )(gn, o_na, proj, proj, w_ssd_o, w_na_o)


def _rope_tables(n_lat, n_ctx):
    t = jnp.arange(n_lat, dtype=jnp.int32)
    n_ax = SSD_STATE // 2
    inv = ROPE_BASE ** (-jnp.arange(0, n_ax, 2, dtype=F32) / n_ax)
    ang_r = (t // GRID_W).astype(F32)[:, None] * inv
    ang_c = (t % GRID_W).astype(F32)[:, None] * inv
    cos_l = jnp.concatenate([jnp.cos(ang_r)] * 2 + [jnp.cos(ang_c)] * 2, axis=1)
    sin_l = jnp.concatenate([-jnp.sin(ang_r), jnp.sin(ang_r), -jnp.sin(ang_c), jnp.sin(ang_c)], axis=1)
    cos_t = jnp.concatenate([cos_l, jnp.ones((n_ctx, LANE), F32)], axis=0)
    sin_t = jnp.concatenate([sin_l, jnp.zeros((n_ctx, LANE), F32)], axis=0)
    return cos_t, sin_t


def _head_selectors():
    sel = np.zeros((2, SSD_GROUPS, LANE, GROUP_W), np.float32)
    for d in range(2):
        for g in range(SSD_GROUPS):
            for e in range(SSD_HPG):
                hd = d * SSD_HEADS + g * SSD_HPG + e
                sel[d, g, hd, e * SSD_HEADDIM:(e + 1) * SSD_HEADDIM] = 1.0
    return jnp.asarray(sel, BF16)


def kernel(x, c, ctx, c_ctx, w_ada, b_ada, g_pre_mix, g_post_mix, g_pre_mlp, g_post_mlp, w_in,
           conv_w, conv_b, a_log, dt_bias, d_skip, ssd_norm, w_ssd_o, rpb, w_na_o, w_out,
           w_mlp1, w_mlp2):
    assert x.shape[0] == 1 and x.shape[2] == D_MODEL
    depth = w_ada.shape[0]
    n_lat = x.shape[1]
    n_ctx = ctx.shape[1]
    n_tot = n_lat + n_ctx
    assert n_lat % (GRID_W * NA_WIN_H) == 0 and n_ctx % 256 == 0 and n_lat % 256 == 0

    cc = jnp.zeros((SUBLANE, D_MODEL), F32).at[0].set(c[0]).at[1].set(c_ctx)
    mod = _ada_call(cc, w_ada, b_ada)
    cos_t, sin_t = _rope_tables(n_lat, n_ctx)
    sel = _head_selectors()
    shift = _conv_shift_matrix()
    w_out_b = w_out.astype(BF16)
    w_mlp2_b = w_mlp2.astype(BF16)
    dsk_all = jnp.repeat(d_skip[:, 0] + d_skip[:, 1], SSD_HEADDIM, axis=-1)

    def row(v):
        return v.reshape(1, -1)

    stream, u = _stream_call(x.reshape(n_lat, D_MODEL), ctx.reshape(n_ctx, D_MODEL),
                             row(g_pre_mix[0]), mod)
    for l in range(depth):
        last = l == depth - 1
        rows_out = n_lat if last else n_tot
        proj, dt_raw = _inproj_call(u, w_in, l)

        xs = _conv_call(proj, shift, conv_w, conv_b, l, 0, D_INNER, None, None, n_lat=n_lat,
                        rope=False, name="conv_x")
        bc = _conv_call(proj, shift, conv_w, conv_b, l, D_INNER, BC_DIM, cos_t, sin_t, n_lat=n_lat,
                        rope=True, name="conv_bc")
        dtp = _dt_call(dt_raw, dt_bias, a_log, l)
        y_f = _ssd_call(xs, bc, dtp, sel, n_lat=n_lat, bwd=False)
        gn = _ssd_call(xs, bc, dtp, sel, n_lat=n_lat, bwd=True, yf=y_f, z=proj,
                       dsk=row(dsk_all[l]), nw=row(ssd_norm[l]))

        o_na = _na_call(proj, _na_bias_pairs(rpb[l]), n_lat=n_lat)
        merged = _gate_call(gn, o_na, proj, w_ssd_o, w_na_o, l, rows=rows_out)
        stream_mid, h = _mm_res_call(merged, w_out_b, stream, row(g_post_mix[l]), mod, l, 2,
                                     1024, rows=rows_out, n_lat=n_lat, name="out_proj",
                                     nxt=(row(g_pre_mlp[l]), l, 3, 4))
        hidden = _mlp_up_call(h, w_mlp1, l, rows=rows_out)
        nxt = None if last else (row(g_pre_mix[l + 1]), l + 1, 0, 1)
        res = _mm_res_call(hidden, w_mlp2_b, stream_mid, row(g_post_mlp[l]), mod, l, 5, 256,
                           rows=rows_out, n_lat=n_lat, name="mlp_down", nxt=nxt)
        stream, u = (res, None) if last else res
    return stream[None]
```

```python
import functools

import numpy as np
import jax
import jax.numpy as jnp
from jax import lax
from jax.experimental import pallas as pl
from jax.experimental.pallas import tpu as pltpu

F32 = jnp.float32
BF16 = jnp.bfloat16

EPS = 1e-6
LOG2E = 1.4426950408889634
ROPE_BASE = 10000.0
GRID_W = 64
D_MODEL = 2048
D_INNER = 2 * D_MODEL
SSD_HEADDIM = 64
SSD_HEADS = D_INNER // SSD_HEADDIM
SSD_GROUPS = 8
SSD_HPG = SSD_HEADS // SSD_GROUPS
SSD_STATE = 128
SSD_CONV = 5
SSD_CHUNK = 128
GROUP_W = SSD_HPG * SSD_HEADDIM
BC_DIM = 2 * SSD_GROUPS * SSD_STATE
CONV_DIM = D_INNER + BC_DIM
NA_HEAD_DIM = 128
NA_HEADS = D_MODEL // NA_HEAD_DIM
NA_WIN_H = 8
NA_WIN_W = 16
D_FF = 4 * D_MODEL
OFF_Z = 0
OFF_XBC = D_INNER
OFF_DT = OFF_XBC + CONV_DIM
OFF_QKVG = OFF_DT + 2 * SSD_HEADS
QKVG_DIM = 5 * D_MODEL
PROJ_Z = 0
PROJ_XBC = D_INNER
PROJ_QKVG = PROJ_XBC + CONV_DIM
PROJ_DIM = PROJ_QKVG + QKVG_DIM

LANE = 128
SUBLANE = 8
BF16_ROWS = 16
VMEM_LIMIT_MB = 56


def _cparams(n_axes, vmem_mb=VMEM_LIMIT_MB):
    return pltpu.CompilerParams(dimension_semantics=("arbitrary",) * n_axes,
                                vmem_limit_bytes=vmem_mb * 1024 * 1024)


def _pick_tile(rows, cap):
    best = None
    for d in range(BF16_ROWS, cap + 1, BF16_ROWS):
        if rows % d == 0:
            best = d
    assert best is not None, (rows, cap)
    return best


EPILOGUE_ROWS = BF16_ROWS


def _sigmoid(v):
    return 1.0 / (1.0 + jnp.exp(-v))


def _dot(a, b):
    return jnp.dot(a, b, preferred_element_type=F32)


def _dot_nt(a, b):
    return lax.dot_general(a, b, (((1,), (1,)), ((), ())), preferred_element_type=F32)


def _select_rows(ref, row0, rc, n_lat):
    rows = row0 + lax.broadcasted_iota(jnp.int32, (rc, 1), 0)
    return jnp.where(rows >= n_lat, ref[1:2, :], ref[0:1, :])


def _ada_kernel(c_ref, w_ref, b_ref, o_ref):
    cv = c_ref[...]
    s = cv * _sigmoid(cv)
    o_ref[...] = _dot(s.astype(BF16), w_ref[...].astype(BF16)) + b_ref[...]


def _ada_call(cc, w_ada, b_ada):
    depth, d, n6 = w_ada.shape
    tn = 1024
    return pl.pallas_call(
        _ada_kernel,
        grid=(depth, n6 // tn),
        in_specs=[pl.BlockSpec((SUBLANE, d), lambda l, j: (0, 0)),
                  pl.BlockSpec((None, d, tn), lambda l, j: (l, 0, j)),
                  pl.BlockSpec((None, 1, tn), lambda l, j: (l, 0, j))],
        out_specs=pl.BlockSpec((None, SUBLANE, tn), lambda l, j: (l, 0, j)),
        out_shape=jax.ShapeDtypeStruct((depth, SUBLANE, n6), F32),
        compiler_params=_cparams(2),
        name="ada",
    )(cc, w_ada, b_ada.reshape(depth, 1, n6))


def _norm_modulate(v, g_ref, sh, sc):
    ms = jnp.mean(v * v, axis=-1, keepdims=True)
    return v * lax.rsqrt(ms + EPS) * g_ref[...] * (1.0 + sc) + sh


def _stream_kernel(x_ref, c_ref, g_ref, sh_ref, sc_ref, s_ref, u_ref, *, lat_tiles):
    i = pl.program_id(0)

    def emit(v, row):
        s_ref[...] = v
        u_ref[...] = _norm_modulate(v, g_ref, sh_ref[row:row + 1, :], sc_ref[row:row + 1, :]
                                    ).astype(u_ref.dtype)

    @pl.when(i < lat_tiles)
    def _():
        emit(x_ref[...], 0)

    @pl.when(i >= lat_tiles)
    def _():
        emit(c_ref[...], 1)


def _stream_call(x2, ctx2, g, mod):
    n_lat, d = x2.shape
    n_ctx = ctx2.shape[0]
    tr = 256
    lat_tiles = n_lat // tr
    row = pl.BlockSpec((tr, d), lambda i: (i, 0))
    return pl.pallas_call(
        functools.partial(_stream_kernel, lat_tiles=lat_tiles),
        grid=((n_lat + n_ctx) // tr,),
        in_specs=[pl.BlockSpec((tr, d), lambda i: (jnp.minimum(i, lat_tiles - 1), 0)),
                  pl.BlockSpec((tr, d), lambda i: (jnp.maximum(i - lat_tiles, 0), 0)),
                  pl.BlockSpec((1, d), lambda i: (0, 0)),
                  pl.BlockSpec((None, SUBLANE, d), lambda i: (0, 0, 0)),
                  pl.BlockSpec((None, SUBLANE, d), lambda i: (0, 0, 1))],
        out_specs=[row, row],
        out_shape=[jax.ShapeDtypeStruct((n_lat + n_ctx, d), F32),
                   jax.ShapeDtypeStruct((n_lat + n_ctx, d), BF16)],
        compiler_params=_cparams(1),
        name="stream",
    )(x2, ctx2, g, mod, mod)


def _inproj_kernel(a_ref, w_ref, wdt_ref, p_ref, dt_ref):
    a = a_ref[...]
    p_ref[...] = _dot(a, w_ref[0].astype(BF16)).astype(p_ref.dtype)

    @pl.when(pl.program_id(1) == 0)
    def _():
        dt_ref[...] = _dot(a, wdt_ref[...].astype(BF16))


def _inproj_call(u, w_in, layer):
    rows, d = u.shape
    tm = _pick_tile(rows, 2112)
    tn = 512
    dtw = 2 * SSD_HEADS

    def wcol(i, j):
        return (layer, 0, pl.multiple_of(j * tn + jnp.where(j >= PROJ_QKVG // tn, dtw, 0), dtw))
    return pl.pallas_call(
        _inproj_kernel,
        grid=(rows // tm, PROJ_DIM // tn),
        in_specs=[pl.BlockSpec((tm, d), lambda i, j: (i, 0)),
                  pl.BlockSpec((pl.Element(1), pl.Element(d), pl.Element(tn)), wcol),
                  pl.BlockSpec((None, d, dtw), lambda i, j: (layer, 0, OFF_DT // dtw))],
        out_specs=[pl.BlockSpec((tm, tn), lambda i, j: (i, j)),
                   pl.BlockSpec((tm, dtw), lambda i, j: (i, 0))],
        out_shape=[jax.ShapeDtypeStruct((rows, PROJ_DIM), BF16),
                   jax.ShapeDtypeStruct((rows, dtw), F32)],
        compiler_params=_cparams(2),
        name="inproj",
    )(u, w_in, w_in)


def _mlp_up_kernel(a_ref, w_ref, o_ref):
    acc = _dot(a_ref[...], w_ref[...].astype(BF16))
    o_ref[...] = jnp.square(jnp.maximum(acc, 0.0)).astype(o_ref.dtype)


def _mlp_up_call(h, w1, layer, *, rows):
    d = h.shape[1]
    dff = w1.shape[2]
    tm = _pick_tile(rows, 2112)
    tn = 512
    return pl.pallas_call(
        _mlp_up_kernel,
        grid=(rows // tm, dff // tn),
        in_specs=[pl.BlockSpec((tm, d), lambda i, j: (i, 0)),
                  pl.BlockSpec((None, d, tn), lambda i, j: (layer, 0, j))],
        out_specs=pl.BlockSpec((tm, tn), lambda i, j: (i, j)),
        out_shape=jax.ShapeDtypeStruct((rows, dff), BF16),
        compiler_params=_cparams(2),
        name="mlp_up",
    )(h, w1)


def _mm_res_kernel(*refs, tm, nj, n_lat, emit_next, all_latent):
    if emit_next:
        (a_ref, w_ref, x_ref, g_ref, gt_ref, g2_ref, sh_ref, sc_ref, o_ref, u_ref, acc_ref) = refs
    else:
        (a_ref, w_ref, x_ref, g_ref, gt_ref, o_ref, acc_ref) = refs
    i = pl.program_id(0)
    j = pl.program_id(1)
    acc_ref[j] = _dot(a_ref[...], w_ref[...])
    rc = EPILOGUE_ROWS
    tile_lo = i * tm

    def epilogue(pick):
        def body(q, carry):
            r0 = pl.multiple_of(q * rc, rc)
            y = jnp.concatenate([acc_ref[jj, pl.ds(r0, rc), :] for jj in range(nj)], axis=1)
            ms = jnp.mean(y * y, axis=-1, keepdims=True)
            r = y * lax.rsqrt(ms + EPS) * g_ref[...]
            xn = x_ref[pl.ds(r0, rc), :] + pick(gt_ref, r0) * r
            o_ref[pl.ds(r0, rc), :] = xn
            if emit_next:
                u_ref[pl.ds(r0, rc), :] = _norm_modulate(
                    xn, g2_ref, pick(sh_ref, r0), pick(sc_ref, r0)).astype(u_ref.dtype)
            return carry
        lax.fori_loop(0, tm // rc, body, 0, unroll=3 if (tm // rc) % 3 == 0 else 2)

    last = j == nj - 1
    if all_latent:
        pl.when(last)(lambda: epilogue(lambda ref, r0: ref[0:1, :]))
    else:
        side = (tile_lo >= n_lat).astype(jnp.int32)
        mixed = jnp.logical_and(tile_lo < n_lat, tile_lo + tm > n_lat)
        pl.when(jnp.logical_and(last, jnp.logical_not(mixed)))(
            lambda: epilogue(lambda ref, r0: ref[pl.ds(side, 1), :]))
        pl.when(jnp.logical_and(last, mixed))(
            lambda: epilogue(lambda ref, r0: _select_rows(ref, tile_lo + r0, rc, n_lat)))


def _mm_res_call(a, w_bf16, x, g, mod, layer, gt_idx, tn, *, rows, n_lat, name, nxt=None):
    _, kdim, d = w_bf16.shape
    tm = _pick_tile(rows, 528)
    nj = d // tn
    emit_next = nxt is not None
    kern = functools.partial(_mm_res_kernel, tm=tm, nj=nj, n_lat=n_lat, emit_next=emit_next,
                             all_latent=rows <= n_lat)
    in_specs = [pl.BlockSpec((tm, kdim), lambda i, k: (i, 0)),
                pl.BlockSpec((None, kdim, tn), lambda i, k: (layer, 0, k)),
                pl.BlockSpec((tm, d), lambda i, k: (i, 0)),
                pl.BlockSpec((1, d), lambda i, k: (0, 0)),
                pl.BlockSpec((None, SUBLANE, d), lambda i, k: (layer, 0, gt_idx))]
    args = [a, w_bf16, x, g, mod]
    row = pl.BlockSpec((tm, d), lambda i, k: (i, 0))
    out_specs = row
    out_shape = jax.ShapeDtypeStruct((rows, d), F32)
    if emit_next:
        g2, layer2, sh_idx, sc_idx = nxt
        in_specs += [pl.BlockSpec((1, d), lambda i, k: (0, 0)),
                     pl.BlockSpec((None, SUBLANE, d), lambda i, k: (layer2, 0, sh_idx)),
                     pl.BlockSpec((None, SUBLANE, d), lambda i, k: (layer2, 0, sc_idx))]
        args += [g2, mod, mod]
        out_specs = [row, row]
        out_shape = [out_shape, jax.ShapeDtypeStruct((rows, d), BF16)]
    return pl.pallas_call(
        kern,
        grid=(rows // tm, nj),
        in_specs=in_specs,
        out_specs=out_specs,
        out_shape=out_shape,
        scratch_shapes=[pltpu.VMEM((nj, tm, tn), F32)],
        compiler_params=_cparams(2),
        name=name,
    )(*args)


CONV_RB = 128

def _conv_kernel(*refs, tt, n_lat, n_tot, rope):
    if rope:
        prev_ref, cur_ref, next_ref, sm_ref, w_ref, b_ref, cos_ref, sin_ref, o_ref = refs
    else:
        prev_ref, cur_ref, next_ref, sm_ref, w_ref, b_ref, o_ref = refs
    s0 = pl.program_id(0) * tt
    first = jnp.logical_or(s0 == 0, s0 == n_lat)
    last = jnp.logical_or(s0 + tt == n_lat, s0 + tt == n_tot)
    zero = jnp.zeros_like(prev_ref[...])
    xe = jnp.concatenate([jnp.where(first, zero, prev_ref[...]), cur_ref[...],
                          jnp.where(last, zero, next_ref[...])], axis=0)
    tc = cur_ref.shape[1]
    rb, cbw = CONV_RB, 512
    half = SSD_CONV // 2
    taps = [k for k in range(SSD_CONV) if k != half]
    wb = w_ref[...].astype(BF16)
    for r0 in range(0, tt, rb):
        win = xe[r0:r0 + rb + 2 * BF16_ROWS, :]
        stacked = jnp.concatenate([win * wb[k:k + 1, :] for k in taps], axis=0)
        sh = _dot(sm_ref[...], stacked)
        for c0 in range(0, tc, cbw):
            acc = (b_ref[:, c0:c0 + cbw] + w_ref[half:half + 1, c0:c0 + cbw] * cur_ref[
                r0:r0 + rb, c0:c0 + cbw].astype(F32)) + sh[:, c0:c0 + cbw]
            y = acc * _sigmoid(acc)
            if rope:
                cosv = cos_ref[r0:r0 + rb, :]
                sinv = sin_ref[r0:r0 + rb, :]
                lane = lax.broadcasted_iota(jnp.int32, (rb, LANE), 1)
                low_half = (lane % 64) < 32
                for s in range(cbw // LANE):
                    yb = y[:, s * LANE:(s + 1) * LANE]
                    sw = jnp.where(low_half, pltpu.roll(yb, LANE - 32, 1), pltpu.roll(yb, 32, 1))
                    o_ref[r0:r0 + rb, c0 + s * LANE:c0 + (s + 1) * LANE] = (
                        yb * cosv + sw * sinv).astype(o_ref.dtype)
            else:
                o_ref[r0:r0 + rb, c0:c0 + cbw] = y.astype(o_ref.dtype)


def _conv_call(proj, shift, conv_w, conv_b, layer, col0, ncols, cos_t, sin_t, *, n_lat, rope, name):
    n_tot = proj.shape[0]
    tt, tc = 256, min(ncols, 4096)
    assert n_lat % tt == 0 and n_tot % tt == 0 and col0 % tc == 0 and ncols % tc == 0
    assert (PROJ_XBC + col0) % tc == 0
    cb = col0 // tc
    pb = (PROJ_XBC + col0) // tc
    hb = tt // BF16_ROWS
    nhb = n_tot // BF16_ROWS
    kern = functools.partial(_conv_kernel, tt=tt, n_lat=n_lat, n_tot=n_tot, rope=rope)
    in_specs = [pl.BlockSpec((BF16_ROWS, tc), lambda i, j: (jnp.maximum(i * hb - 1, 0), pb + j)),
                pl.BlockSpec((tt, tc), lambda i, j: (i, pb + j)),
                pl.BlockSpec((BF16_ROWS, tc), lambda i, j: (jnp.minimum((i + 1) * hb, nhb - 1), pb + j)),
                pl.BlockSpec(shift.shape, lambda i, j: (0, 0)),
                pl.BlockSpec((None, SSD_CONV, tc), lambda i, j: (layer, 0, cb + j)),
                pl.BlockSpec((None, 1, tc), lambda i, j: (layer, 0, cb + j))]
    args = [proj, proj, proj, shift, conv_w, conv_b.reshape(conv_b.shape[0], 1, -1)]
    if rope:
        in_specs += [pl.BlockSpec((tt, LANE), lambda i, j: (i, 0)),
                     pl.BlockSpec((tt, LANE), lambda i, j: (i, 0))]
        args += [cos_t, sin_t]
    return pl.pallas_call(
        kern,
        grid=(n_tot // tt, ncols // tc),
        in_specs=in_specs,
        out_specs=pl.BlockSpec((tt, tc), lambda i, j: (i, j)),
        out_shape=jax.ShapeDtypeStruct((n_tot, ncols), BF16),
        compiler_params=_cparams(2),
        name=name,
    )(*args)


def _conv_shift_matrix():
    half = SSD_CONV // 2
    offs = [k - half for k in range(SSD_CONV) if k != half]
    win = CONV_RB + 2 * BF16_ROWS
    m = np.zeros((CONV_RB, len(offs) * win), np.float32)
    for t, o in enumerate(offs):
        m[np.arange(CONV_RB), t * win + BF16_ROWS + np.arange(CONV_RB) + o] = 1.0
    return jnp.asarray(m, BF16)


def _dt_kernel(raw_ref, bias_ref, alog_ref, cs_ref, w_ref, ecs_ref, cst_ref):
    lane = lax.broadcasted_iota(jnp.int32, (LANE, SSD_CHUNK), 1)
    row = lax.broadcasted_iota(jnp.int32, (LANE, SSD_CHUNK), 0)
    col = lax.broadcasted_iota(jnp.int32, (1, LANE), 1)
    for c in range(DT_CHUNKS):
        tok = slice(c * SSD_CHUNK, (c + 1) * SSD_CHUNK)
        raw = raw_ref[tok, :] + bias_ref[...]
        dt = jnp.maximum(raw, 0.0) + jnp.log1p(jnp.exp(-jnp.abs(raw)))
        da_t = (dt * (-jnp.exp(alog_ref[...]))).T
        pre = da_t
        suf = da_t
        k = 1
        while k < SSD_CHUNK:
            pre = pre + jnp.where(lane >= k, pltpu.roll(pre, k, 1), 0.0)
            suf = suf + jnp.where(lane < SSD_CHUNK - k, pltpu.roll(suf, SSD_CHUNK - k, 1), 0.0)
            k *= 2
        cst = jnp.where(row < SSD_HEADS, pre, suf)
        cs = cst.T
        end = jnp.where(col < SSD_HEADS, cs[SSD_CHUNK - 1:SSD_CHUNK, :], cs[0:1, :])
        cs_ref[tok, :] = cs * LOG2E
        w_ref[tok, :] = jnp.exp(end - cs) * dt
        ecs_ref[tok, :] = jnp.exp(cs)
        cst_ref[:, tok] = cst * LOG2E - jnp.log2(dt).T


DT_CHUNKS = 2


def _dt_call(dt_raw, dt_bias, a_log, layer):
    n_tot = dt_raw.shape[0]
    span = DT_CHUNKS * SSD_CHUNK
    assert n_tot % span == 0
    row_major = pl.BlockSpec((span, LANE), lambda c: (c, 0))
    head_major = pl.BlockSpec((LANE, span), lambda c: (0, c))
    par = pl.BlockSpec((None, 1, LANE), lambda c: (layer, 0, 0))
    return pl.pallas_call(
        _dt_kernel,
        grid=(n_tot // span,),
        in_specs=[row_major, par, par],
        out_specs=[row_major, row_major, row_major, head_major],
        out_shape=[jax.ShapeDtypeStruct((n_tot, LANE), F32)] * 3
        + [jax.ShapeDtypeStruct((LANE, n_tot), F32)],
        compiler_params=_cparams(1),
        name="ssd_dt",
    )(dt_raw, dt_bias.reshape(-1, 1, LANE), a_log.reshape(-1, 1, LANE))


def _split_cat(v, parts):
    out = []
    rem = v
    for _ in range(parts):
        p = rem.astype(BF16)
        out.append(p)
        rem = rem - p.astype(F32)
    return jnp.concatenate(out, axis=1)


def _ssd_kernel(*refs, bwd):
    if bwd:
        (x_ref, bc_ref, cst_ref, cs_ref, w_ref, ecs_ref, sel_ref,
         yf_ref, z_ref, dsk_ref, nw_ref, o_ref, h_ref) = refs
    else:
        (x_ref, bc_ref, cst_ref, cs_ref, w_ref, ecs_ref, sel_ref, o_ref, h_ref) = refs

    @pl.when(pl.program_id(0) == 0)
    def _():
        h_ref[...] = jnp.zeros_like(h_ref)

    w_sp = _split_cat(w_ref[...], 2)
    e_sp = _split_cat(ecs_ref[...], 2)
    li = lax.broadcasted_iota(jnp.int32, (SSD_CHUNK, SSD_CHUNK), 0)
    si = lax.broadcasted_iota(jnp.int32, (SSD_CHUNK, SSD_CHUNK), 1)
    mask = (li <= si) if bwd else (li >= si)
    lane = lax.broadcasted_iota(jnp.int32, (SSD_CHUNK, LANE), 1)
    exit_row = 0 if bwd else SSD_CHUNK - 1
    dir_col = SSD_HEADS if bwd else 0

    for gi in range(SSD_GROUPS):
        gsl = slice(gi * GROUP_W, (gi + 1) * GROUP_W)
        x = x_ref[:, gsl]
        bm = bc_ref[:, gi * SSD_STATE:(gi + 1) * SSD_STATE]
        cm = bc_ref[:, (SSD_GROUPS + gi) * SSD_STATE:(SSD_GROUPS + gi + 1) * SSD_STATE]
        sel = sel_ref[gi]
        w_x = _dot(w_sp, jnp.concatenate([sel, sel], axis=0))
        e_x = _dot(e_sp, jnp.concatenate([sel, sel], axis=0))
        col0 = dir_col + gi * SSD_HPG
        cs8 = cs_ref[:, col0:col0 + SSD_HPG]

        cb = _dot_nt(cm, bm)
        h = h_ref[gi]
        y_off = _dot(cm, h.astype(BF16)) * e_x
        xw = (x.astype(F32) * w_x).astype(BF16)
        bt = bm.astype(F32).T.astype(BF16)
        h_ref[gi] = h * e_x[exit_row:exit_row + 1, :] + _dot(bt, xw)

        cst = cst_ref[gi * SSD_HPG:(gi + 1) * SSD_HPG, :]
        ys = []
        for j in range(SSD_HPG // 2):
            ms = []
            for e in (2 * j, 2 * j + 1):
                diff = cs8[:, e:e + 1] - cst[e:e + 1, :]
                m = jnp.where(mask, jnp.exp2(diff), 0.0) * cb
                ms.append(m.astype(BF16))
            xp = x[:, j * LANE:(j + 1) * LANE]
            zero = jnp.zeros_like(xp)
            x2 = jnp.concatenate([jnp.where(lane < SSD_HEADDIM, xp, zero),
                                  jnp.where(lane >= SSD_HEADDIM, xp, zero)], axis=0)
            ys.append(_dot(jnp.concatenate(ms, axis=1), x2))
        y = y_off + jnp.concatenate(ys, axis=1)

        if bwd:
            yt = y + yf_ref[:, gsl] + dsk_ref[:, gsl] * x.astype(F32)
            zz = z_ref[:, gsl].astype(F32)
            g = yt * (zz * _sigmoid(zz))
            gs = g * lax.rsqrt(jnp.mean(g * g, axis=-1, keepdims=True) + EPS)
            o_ref[:, gsl] = (gs * nw_ref[:, gsl]).astype(o_ref.dtype)
        else:
            o_ref[:, gsl] = y


def _ssd_call(xs, bc, dtp, sel, *, n_lat, bwd, yf=None, z=None, dsk=None, nw=None):
    cs, w, ecs, cst = dtp
    n_tot = xs.shape[0]
    ncl = n_lat // SSD_CHUNK
    nch = n_tot // SSD_CHUNK
    ncc = nch - ncl
    d = 1 if bwd else 0
    if bwd:
        def cid(s):
            return nch - 1 - s
    else:
        def cid(s):
            return jnp.where(s < ncc, ncl + s, s - ncc)
    tile = pl.BlockSpec((SSD_CHUNK, D_INNER), lambda s: (cid(s), 0))
    full = pl.BlockSpec((SSD_CHUNK, LANE), lambda s: (cid(s), 0))
    heads = pl.BlockSpec((SSD_HEADS, SSD_CHUNK), lambda s: (d, cid(s)))
    in_specs = [tile,
                pl.BlockSpec((SSD_CHUNK, BC_DIM), lambda s: (cid(s), 0)),
                heads, full, full, full,
                pl.BlockSpec((None, SSD_GROUPS, LANE, GROUP_W), lambda s: (d, 0, 0, 0))]
    args = [xs, bc, cst, cs, w, ecs, sel]
    if bwd:
        in_specs += [tile, tile,
                     pl.BlockSpec((1, D_INNER), lambda s: (0, 0)),
                     pl.BlockSpec((1, D_INNER), lambda s: (0, 0))]
        args += [yf, z, dsk, nw]
    return pl.pallas_call(
        functools.partial(_ssd_kernel, bwd=bwd),
        grid=(nch,),
        in_specs=in_specs,
        out_specs=tile,
        out_shape=jax.ShapeDtypeStruct((n_tot, D_INNER), BF16 if bwd else F32),
        scratch_shapes=[pltpu.VMEM((SSD_GROUPS, SSD_STATE, GROUP_W), F32)],
        compiler_params=_cparams(1),
        name="ssd_bwd" if bwd else "ssd_fwd",
    )(*args)


NA_QROWS = 4
NA_KROWS = NA_QROWS + NA_WIN_H
NA_TOK_BLK = 128


def _na_fill_bias(pp_ref, bias_ref):
    lane = lax.broadcasted_iota(jnp.int32, (GRID_W, 2 * GRID_W), 1)
    ninf = jnp.full((GRID_W, 2 * GRID_W), -jnp.inf, F32)
    for v in range(3):
        for kk in range(NA_KROWS):
            for p in range(NA_QROWS // 2):
                i0 = 2 * p
                if v == 1:
                    ok0, ok1 = 0 <= kk - i0 < NA_WIN_H, 0 <= kk - i0 - 1 < NA_WIN_H
                    d0 = kk - i0 + NA_WIN_H // 2 - 1
                else:
                    ok0 = ok1 = kk < NA_WIN_H
                    d0 = kk - i0 + (NA_WIN_H - 1 if v == 0 else NA_WIN_H // 2 - 1)
                if ok0 and ok1:
                    blk = pp_ref[d0]
                elif ok0:
                    blk = jnp.where(lane < GRID_W, pp_ref[d0], ninf)
                elif ok1:
                    blk = jnp.where(lane >= GRID_W, pp_ref[d0], ninf)
                else:
                    blk = ninf
                bias_ref[v, kk * GRID_W:(kk + 1) * GRID_W, p * 2 * GRID_W:(p + 1) * 2 * GRID_W] = blk


def _na_kernel(q_ref, k_ref, v_ref, pp_ref, o_ref, qt_ref, vt_ref, bias_ref, *, n_lat, n_ctx):
    _na_fill_bias(pp_ref, bias_ref)
    rows = n_lat // GRID_W
    qscale = NA_HEAD_DIM ** -0.5 * LOG2E
    nq = NA_QROWS * GRID_W
    nkeys = NA_KROWS * GRID_W
    qblk = nq // NA_TOK_BLK
    kblk = nkeys // NA_TOK_BLK
    lat_blk = n_lat // NA_TOK_BLK
    ctx_blk = n_ctx // NA_TOK_BLK

    def transpose_blocks(p, carry):
        t0 = pl.multiple_of(p * NA_TOK_BLK, NA_TOK_BLK)
        qt_ref[p] = (q_ref[pl.ds(t0, NA_TOK_BLK), :].astype(F32) * qscale).astype(BF16).T
        vt_ref[p] = v_ref[pl.ds(t0, NA_TOK_BLK), :].T
        return carry

    lax.fori_loop(0, lat_blk + ctx_blk, transpose_blocks, 0, unroll=6)

    kc = k_ref[n_lat:n_lat + n_ctx, :]
    vtc = jnp.concatenate([vt_ref[lat_blk + b] for b in range(ctx_blk)], axis=1)

    def softmax_pv(parts):
        m = parts[0][0].max(axis=0, keepdims=True)
        for s, _ in parts[1:]:
            m = jnp.maximum(m, s.max(axis=0, keepdims=True))
        den = 0.0
        acc = 0.0
        for s, vt in parts:
            p = jnp.exp2(s - m)
            den = den + p.sum(axis=0, keepdims=True)
            acc = acc + _dot(vt, p.astype(BF16))
        return acc / den

    def body(t, carry):
        r0 = t * NA_QROWS
        ws = jnp.clip(r0 - NA_WIN_H // 2, 0, rows - NA_WIN_H)
        variant = jnp.where(r0 == 0, 0, jnp.where(r0 == rows - NA_QROWS, 2, 1))
        q0 = pl.multiple_of(r0 * GRID_W, nq)
        k0 = pl.multiple_of(ws * GRID_W, nq)
        kb0 = ws * GRID_W // NA_TOK_BLK
        qt = jnp.concatenate([qt_ref[t * qblk + b] for b in range(qblk)], axis=1)
        vt = jnp.concatenate([vt_ref[kb0 + b] for b in range(kblk)], axis=1)
        s_loc = _dot(k_ref[pl.ds(k0, nkeys), :], qt) + bias_ref[variant]
        s_ctx = _dot(kc, qt)
        ot = softmax_pv([(s_loc, vt), (s_ctx, vtc)])
        o_ref[pl.ds(q0, nq), :] = ot.T.astype(o_ref.dtype)
        return carry

    lax.fori_loop(0, rows // NA_QROWS, body, 0, unroll=4)
    qtc = jnp.concatenate([qt_ref[lat_blk + b] for b in range(ctx_blk)], axis=1)
    otc = softmax_pv([(_dot(kc, qtc), vtc)])
    o_ref[n_lat:n_lat + n_ctx, :] = otc.T.astype(o_ref.dtype)


def _na_call(proj, pairs, *, n_lat):
    n_tot = proj.shape[0]
    n_ctx = n_tot - n_lat
    nblk = n_tot // NA_TOK_BLK
    assert n_ctx >= NA_QROWS * GRID_W
    q0 = PROJ_QKVG // NA_HEAD_DIM

    def col(base):
        return pl.BlockSpec((n_tot, NA_HEAD_DIM), lambda h: (0, base + h))
    return pl.pallas_call(
        functools.partial(_na_kernel, n_lat=n_lat, n_ctx=n_ctx),
        grid=(NA_HEADS,),
        in_specs=[col(q0), col(q0 + NA_HEADS), col(q0 + 2 * NA_HEADS),
                  pl.BlockSpec((None, 2 * NA_WIN_H - 1, GRID_W, 2 * GRID_W), lambda h: (h, 0, 0, 0))],
        out_specs=col(0),
        out_shape=jax.ShapeDtypeStruct((n_tot, D_MODEL), BF16),
        scratch_shapes=[pltpu.VMEM((nblk, NA_HEAD_DIM, NA_TOK_BLK), BF16),
                        pltpu.VMEM((nblk, NA_HEAD_DIM, NA_TOK_BLK), BF16),
                        pltpu.VMEM((3, NA_KROWS * GRID_W, NA_QROWS * GRID_W), F32)],
        compiler_params=_cparams(1),
        name="na_attn",
    )(proj, proj, proj, pairs)


def _na_bias_pairs(rpb):
    c = np.arange(GRID_W)[:, None]
    kc = np.arange(GRID_W)[None, :]
    cstart = np.clip(c - NA_WIN_W // 2, 0, GRID_W - NA_WIN_W)
    col_ok = (kc >= cstart) & (kc < cstart + NA_WIN_W)
    dcol = np.clip(kc - c, -(NA_WIN_W - 1), NA_WIN_W - 1) + NA_WIN_W - 1
    onehot = (dcol.T[None] == np.arange(2 * NA_WIN_W - 1)[:, None, None]).astype(np.float32)
    a = jnp.einsum("hdj,jkc->hdkc", rpb, jnp.asarray(onehot), precision=lax.Precision.HIGHEST)
    a = jnp.where(col_ok.T[None, None], a * LOG2E, -jnp.inf).astype(F32)
    prev = jnp.concatenate([jnp.full_like(a[:, :1], -jnp.inf), a[:, :-1]], axis=1)
    return jnp.concatenate([a, prev], axis=-1)


def _gate_kernel(gn_ref, o_ref_in, ga_ref, gb_ref, w1_ref, w2_ref, out_ref):
    y1 = _dot(gn_ref[...], w1_ref[...].astype(BF16))
    y2 = _dot(o_ref_in[...], w2_ref[...].astype(BF16))
    out = _sigmoid(ga_ref[...].astype(F32)) * y1 + _sigmoid(gb_ref[...].astype(F32)) * y2
    out_ref[...] = out.astype(out_ref.dtype)


def _gate_call(gn, o_na, proj, w_ssd_o, w_na_o, layer, *, rows):
    tm = _pick_tile(rows, 1408)
    tn = 256
    ga0 = (PROJ_QKVG + 3 * D_MODEL) // tn
    gb0 = (PROJ_QKVG + 4 * D_MODEL) // tn
    return pl.pallas_call(
        _gate_kernel,
        grid=(rows // tm, D_MODEL // tn),
        in_specs=[pl.BlockSpec((tm, D_INNER), lambda i, j: (i, 0)),
                  pl.BlockSpec((tm, D_MODEL), lambda i, j: (i, 0)),
                  pl.BlockSpec((tm, tn), lambda i, j: (i, ga0 + j)),
                  pl.BlockSpec((tm, tn), lambda i, j: (i, gb0 + j)),
                  pl.BlockSpec((None, D_INNER, tn), lambda i, j: (layer, 0, j)),
                  pl.BlockSpec((None, D_MODEL, tn), lambda i, j: (layer, 0, j))],
        out_specs=pl.BlockSpec((tm, tn), lambda i, j: (i, j)),
        out_shape=jax.ShapeDtypeStruct((rows, D_MODEL), BF16),
        compiler_params=_cparams(2),
        name="gate_merge",
    )(gn, o_na, proj, proj, w_ssd_o, w_na_o)


def _rope_tables(n_lat, n_ctx):
    t = jnp.arange(n_lat, dtype=jnp.int32)
    n_ax = SSD_STATE // 2
    inv = ROPE_BASE ** (-jnp.arange(0, n_ax, 2, dtype=F32) / n_ax)
    ang_r = (t // GRID_W).astype(F32)[:, None] * inv
    ang_c = (t % GRID_W).astype(F32)[:, None] * inv
    cos_l = jnp.concatenate([jnp.cos(ang_r)] * 2 + [jnp.cos(ang_c)] * 2, axis=1)
    sin_l = jnp.concatenate([-jnp.sin(ang_r), jnp.sin(ang_r), -jnp.sin(ang_c), jnp.sin(ang_c)], axis=1)
    cos_t = jnp.concatenate([cos_l, jnp.ones((n_ctx, LANE), F32)], axis=0)
    sin_t = jnp.concatenate([sin_l, jnp.zeros((n_ctx, LANE), F32)], axis=0)
    return cos_t, sin_t


def _head_selectors():
    sel = np.zeros((2, SSD_GROUPS, LANE, GROUP_W), np.float32)
    for d in range(2):
        for g in range(SSD_GROUPS):
            for e in range(SSD_HPG):
                hd = d * SSD_HEADS + g * SSD_HPG + e
                sel[d, g, hd, e * SSD_HEADDIM:(e + 1) * SSD_HEADDIM] = 1.0
    return jnp.asarray(sel, BF16)


def kernel(x, c, ctx, c_ctx, w_ada, b_ada, g_pre_mix, g_post_mix, g_pre_mlp, g_post_mlp, w_in,
           conv_w, conv_b, a_log, dt_bias, d_skip, ssd_norm, w_ssd_o, rpb, w_na_o, w_out,
           w_mlp1, w_mlp2):
    assert x.shape[0] == 1 and x.shape[2] == D_MODEL
    depth = w_ada.shape[0]
    n_lat = x.shape[1]
    n_ctx = ctx.shape[1]
    n_tot = n_lat + n_ctx
    assert n_lat % (GRID_W * NA_WIN_H) == 0 and n_ctx % 256 == 0 and n_lat % 256 == 0

    cc = jnp.zeros((SUBLANE, D_MODEL), F32).at[0].set(c[0]).at[1].set(c_ctx)
    mod = _ada_call(cc, w_ada, b_ada)
    cos_t, sin_t = _rope_tables(n_lat, n_ctx)
    sel = _head_selectors()
    shift = _conv_shift_matrix()
    w_out_b = w_out.astype(BF16)
    w_mlp2_b = w_mlp2.astype(BF16)
    dsk_all = jnp.repeat(d_skip[:, 0] + d_skip[:, 1], SSD_HEADDIM, axis=-1)

    def row(v):
        return v.reshape(1, -1)

    stream, u = _stream_call(x.reshape(n_lat, D_MODEL), ctx.reshape(n_ctx, D_MODEL),
                             row(g_pre_mix[0]), mod)
    for l in range(depth):
        last = l == depth - 1
        rows_out = n_lat if last else n_tot
        proj, dt_raw = _inproj_call(u, w_in, l)

        xs = _conv_call(proj, shift, conv_w, conv_b, l, 0, D_INNER, None, None, n_lat=n_lat,
                        rope=False, name="conv_x")
        bc = _conv_call(proj, shift, conv_w, conv_b, l, D_INNER, BC_DIM, cos_t, sin_t, n_lat=n_lat,
                        rope=True, name="conv_bc")
        dtp = _dt_call(dt_raw, dt_bias, a_log, l)
        y_f = _ssd_call(xs, bc, dtp, sel, n_lat=n_lat, bwd=False)
        gn = _ssd_call(xs, bc, dtp, sel, n_lat=n_lat, bwd=True, yf=y_f, z=proj,
                       dsk=row(dsk_all[l]), nw=row(ssd_norm[l]))

        o_na = _na_call(proj, _na_bias_pairs(rpb[l]), n_lat=n_lat)
        merged = _gate_call(gn, o_na, proj, w_ssd_o, w_na_o, l, rows=rows_out)
        stream_mid, h = _mm_res_call(merged, w_out_b, stream, row(g_post_mix[l]), mod, l, 2,
                                     1024, rows=rows_out, n_lat=n_lat, name="out_proj",
                                     nxt=(row(g_pre_mlp[l]), l, 3, 4))
        hidden = _mlp_up_call(h, w_mlp1, l, rows=rows_out)
        nxt = None if last else (row(g_pre_mix[l + 1]), l + 1, 0, 1)
        res = _mm_res_call(hidden, w_mlp2_b, stream_mid, row(g_post_mlp[l]), mod, l, 5, 256,
                           rows=rows_out, n_lat=n_lat, name="mlp_down", nxt=nxt)
        stream, u = (res, None) if last else res
    return stream[None]
```

```python
import functools

import numpy as np
import jax
import jax.numpy as jnp
from jax import lax
from jax.experimental import pallas as pl
from jax.experimental.pallas import tpu as pltpu

F32 = jnp.float32
BF16 = jnp.bfloat16

EPS = 1e-6
LOG2E = 1.4426950408889634
ROPE_BASE = 10000.0
GRID_W = 64
D_MODEL = 2048
D_INNER = 2 * D_MODEL
SSD_HEADDIM = 64
SSD_HEADS = D_INNER // SSD_HEADDIM
SSD_GROUPS = 8
SSD_HPG = SSD_HEADS // SSD_GROUPS
SSD_STATE = 128
SSD_CONV = 5
SSD_CHUNK = 128
GROUP_W = SSD_HPG * SSD_HEADDIM
BC_DIM = 2 * SSD_GROUPS * SSD_STATE
CONV_DIM = D_INNER + BC_DIM
NA_HEAD_DIM = 128
NA_HEADS = D_MODEL // NA_HEAD_DIM
NA_WIN_H = 8
NA_WIN_W = 16
D_FF = 4 * D_MODEL
OFF_Z = 0
OFF_XBC = D_INNER
OFF_DT = OFF_XBC + CONV_DIM
OFF_QKVG = OFF_DT + 2 * SSD_HEADS
QKVG_DIM = 5 * D_MODEL
PROJ_Z = 0
PROJ_XBC = D_INNER
PROJ_QKVG = PROJ_XBC + CONV_DIM
PROJ_DIM = PROJ_QKVG + QKVG_DIM

LANE = 128
SUBLANE = 8
BF16_ROWS = 16
VMEM_LIMIT_MB = 56


def _cparams(n_axes, vmem_mb=VMEM_LIMIT_MB):
    return pltpu.CompilerParams(dimension_semantics=("arbitrary",) * n_axes,
                                vmem_limit_bytes=vmem_mb * 1024 * 1024)


def _pick_tile(rows, cap):
    best = None
    for d in range(BF16_ROWS, cap + 1, BF16_ROWS):
        if rows % d == 0:
            best = d
    assert best is not None, (rows, cap)
    return best


EPILOGUE_ROWS = BF16_ROWS


def _sigmoid(v):
    return 1.0 / (1.0 + jnp.exp(-v))


def _dot(a, b):
    return jnp.dot(a, b, preferred_element_type=F32)


def _dot_nt(a, b):
    return lax.dot_general(a, b, (((1,), (1,)), ((), ())), preferred_element_type=F32)


def _select_rows(ref, row0, rc, n_lat):
    rows = row0 + lax.broadcasted_iota(jnp.int32, (rc, 1), 0)
    return jnp.where(rows >= n_lat, ref[1:2, :], ref[0:1, :])


def _ada_kernel(c_ref, w_ref, b_ref, o_ref):
    cv = c_ref[...]
    s = cv * _sigmoid(cv)
    o_ref[...] = _dot(s.astype(BF16), w_ref[...].astype(BF16)) + b_ref[...]


def _ada_call(cc, w_ada, b_ada):
    depth, d, n6 = w_ada.shape
    tn = 1024
    return pl.pallas_call(
        _ada_kernel,
        grid=(depth, n6 // tn),
        in_specs=[pl.BlockSpec((SUBLANE, d), lambda l, j: (0, 0)),
                  pl.BlockSpec((None, d, tn), lambda l, j: (l, 0, j)),
                  pl.BlockSpec((None, 1, tn), lambda l, j: (l, 0, j))],
        out_specs=pl.BlockSpec((None, SUBLANE, tn), lambda l, j: (l, 0, j)),
        out_shape=jax.ShapeDtypeStruct((depth, SUBLANE, n6), F32),
        compiler_params=_cparams(2),
        name="ada",
    )(cc, w_ada, b_ada.reshape(depth, 1, n6))


def _norm_modulate(v, g_ref, sh, sc):
    ms = jnp.mean(v * v, axis=-1, keepdims=True)
    return v * lax.rsqrt(ms + EPS) * g_ref[...] * (1.0 + sc) + sh


def _stream_kernel(x_ref, c_ref, g_ref, sh_ref, sc_ref, s_ref, u_ref, *, lat_tiles):
    i = pl.program_id(0)

    def emit(v, row):
        s_ref[...] = v
        u_ref[...] = _norm_modulate(v, g_ref, sh_ref[row:row + 1, :], sc_ref[row:row + 1, :]
                                    ).astype(u_ref.dtype)

    @pl.when(i < lat_tiles)
    def _():
        emit(x_ref[...], 0)

    @pl.when(i >= lat_tiles)
    def _():
        emit(c_ref[...], 1)


def _stream_call(x2, ctx2, g, mod):
    n_lat, d = x2.shape
    n_ctx = ctx2.shape[0]
    tr = 256
    lat_tiles = n_lat // tr
    row = pl.BlockSpec((tr, d), lambda i: (i, 0))
    return pl.pallas_call(
        functools.partial(_stream_kernel, lat_tiles=lat_tiles),
        grid=((n_lat + n_ctx) // tr,),
        in_specs=[pl.BlockSpec((tr, d), lambda i: (jnp.minimum(i, lat_tiles - 1), 0)),
                  pl.BlockSpec((tr, d), lambda i: (jnp.maximum(i - lat_tiles, 0), 0)),
                  pl.BlockSpec((1, d), lambda i: (0, 0)),
                  pl.BlockSpec((None, SUBLANE, d), lambda i: (0, 0, 0)),
                  pl.BlockSpec((None, SUBLANE, d), lambda i: (0, 0, 1))],
        out_specs=[row, row],
        out_shape=[jax.ShapeDtypeStruct((n_lat + n_ctx, d), F32),
                   jax.ShapeDtypeStruct((n_lat + n_ctx, d), BF16)],
        compiler_params=_cparams(1),
        name="stream",
    )(x2, ctx2, g, mod, mod)


def _inproj_kernel(a_ref, w_ref, wdt_ref, p_ref, dt_ref):
    a = a_ref[...]
    p_ref[...] = _dot(a, w_ref[0].astype(BF16)).astype(p_ref.dtype)

    @pl.when(pl.program_id(1) == 0)
    def _():
        dt_ref[...] = _dot(a, wdt_ref[...].astype(BF16))


def _inproj_call(u, w_in, layer):
    rows, d = u.shape
    tm = _pick_tile(rows, 2112)
    tn = 512
    dtw = 2 * SSD_HEADS

    def wcol(i, j):
        return (layer, 0, pl.multiple_of(j * tn + jnp.where(j >= PROJ_QKVG // tn, dtw, 0), dtw))
    return pl.pallas_call(
        _inproj_kernel,
        grid=(rows // tm, PROJ_DIM // tn),
        in_specs=[pl.BlockSpec((tm, d), lambda i, j: (i, 0)),
                  pl.BlockSpec((pl.Element(1), pl.Element(d), pl.Element(tn)), wcol),
                  pl.BlockSpec((None, d, dtw), lambda i, j: (layer, 0, OFF_DT // dtw))],
        out_specs=[pl.BlockSpec((tm, tn), lambda i, j: (i, j)),
                   pl.BlockSpec((tm, dtw), lambda i, j: (i, 0))],
        out_shape=[jax.ShapeDtypeStruct((rows, PROJ_DIM), BF16),
                   jax.ShapeDtypeStruct((rows, dtw), F32)],
        compiler_params=_cparams(2),
        name="inproj",
    )(u, w_in, w_in)


def _mlp_up_kernel(a_ref, w_ref, o_ref):
    acc = _dot(a_ref[...], w_ref[...].astype(BF16))
    o_ref[...] = jnp.square(jnp.maximum(acc, 0.0)).astype(o_ref.dtype)


def _mlp_up_call(h, w1, layer, *, rows):
    d = h.shape[1]
    dff = w1.shape[2]
    tm = _pick_tile(rows, 2112)
    tn = 512
    return pl.pallas_call(
        _mlp_up_kernel,
        grid=(rows // tm, dff // tn),
        in_specs=[pl.BlockSpec((tm, d), lambda i, j: (i, 0)),
                  pl.BlockSpec((None, d, tn), lambda i, j: (layer, 0, j))],
        out_specs=pl.BlockSpec((tm, tn), lambda i, j: (i, j)),
        out_shape=jax.ShapeDtypeStruct((rows, dff), BF16),
        compiler_params=_cparams(2),
        name="mlp_up",
    )(h, w1)


def _mm_res_kernel(*refs, tm, nj, n_lat, emit_next, all_latent):
    if emit_next:
        (a_ref, w_ref, x_ref, g_ref, gt_ref, g2_ref, sh_ref, sc_ref, o_ref, u_ref, acc_ref) = refs
    else:
        (a_ref, w_ref, x_ref, g_ref, gt_ref, o_ref, acc_ref) = refs
    i = pl.program_id(0)
    j = pl.program_id(1)
    acc_ref[j] = _dot(a_ref[...], w_ref[...])
    rc = EPILOGUE_ROWS
    tile_lo = i * tm

    def epilogue(pick):
        def body(q, carry):
            r0 = pl.multiple_of(q * rc, rc)
            y = jnp.concatenate([acc_ref[jj, pl.ds(r0, rc), :] for jj in range(nj)], axis=1)
            ms = jnp.mean(y * y, axis=-1, keepdims=True)
            r = y * lax.rsqrt(ms + EPS) * g_ref[...]
            xn = x_ref[pl.ds(r0, rc), :] + pick(gt_ref, r0) * r
            o_ref[pl.ds(r0, rc), :] = xn
            if emit_next:
                u_ref[pl.ds(r0, rc), :] = _norm_modulate(
                    xn, g2_ref, pick(sh_ref, r0), pick(sc_ref, r0)).astype(u_ref.dtype)
            return carry
        lax.fori_loop(0, tm // rc, body, 0, unroll=3 if (tm // rc) % 3 == 0 else 2)

    last = j == nj - 1
    if all_latent:
        pl.when(last)(lambda: epilogue(lambda ref, r0: ref[0:1, :]))
    else:
        side = (tile_lo >= n_lat).astype(jnp.int32)
        mixed = jnp.logical_and(tile_lo < n_lat, tile_lo + tm > n_lat)
        pl.when(jnp.logical_and(last, jnp.logical_not(mixed)))(
            lambda: epilogue(lambda ref, r0: ref[pl.ds(side, 1), :]))
        pl.when(jnp.logical_and(last, mixed))(
            lambda: epilogue(lambda ref, r0: _select_rows(ref, tile_lo + r0, rc, n_lat)))


def _mm_res_call(a, w_bf16, x, g, mod, layer, gt_idx, tn, *, rows, n_lat, name, nxt=None):
    _, kdim, d = w_bf16.shape
    tm = _pick_tile(rows, 528)
    nj = d // tn
    emit_next = nxt is not None
    kern = functools.partial(_mm_res_kernel, tm=tm, nj=nj, n_lat=n_lat, emit_next=emit_next,
                             all_latent=rows <= n_lat)
    in_specs = [pl.BlockSpec((tm, kdim), lambda i, k: (i, 0)),
                pl.BlockSpec((None, kdim, tn), lambda i, k: (layer, 0, k)),
                pl.BlockSpec((tm, d), lambda i, k: (i, 0)),
                pl.BlockSpec((1, d), lambda i, k: (0, 0)),
                pl.BlockSpec((None, SUBLANE, d), lambda i, k: (layer, 0, gt_idx))]
    args = [a, w_bf16, x, g, mod]
    row = pl.BlockSpec((tm, d), lambda i, k: (i, 0))
    out_specs = row
    out_shape = jax.ShapeDtypeStruct((rows, d), F32)
    if emit_next:
        g2, layer2, sh_idx, sc_idx = nxt
        in_specs += [pl.BlockSpec((1, d), lambda i, k: (0, 0)),
                     pl.BlockSpec((None, SUBLANE, d), lambda i, k: (layer2, 0, sh_idx)),
                     pl.BlockSpec((None, SUBLANE, d), lambda i, k: (layer2, 0, sc_idx))]
        args += [g2, mod, mod]
        out_specs = [row, row]
        out_shape = [out_shape, jax.ShapeDtypeStruct((rows, d), BF16)]
    return pl.pallas_call(
        kern,
        grid=(rows // tm, nj),
        in_specs=in_specs,
        out_specs=out_specs,
        out_shape=out_shape,
        scratch_shapes=[pltpu.VMEM((nj, tm, tn), F32)],
        compiler_params=_cparams(2),
        name=name,
    )(*args)


CONV_RB = 128

def _conv_kernel(*refs, tt, n_lat, n_tot, rope):
    if rope:
        prev_ref, cur_ref, next_ref, sm_ref, w_ref, b_ref, cos_ref, sin_ref, o_ref = refs
    else:
        prev_ref, cur_ref, next_ref, sm_ref, w_ref, b_ref, o_ref = refs
    s0 = pl.program_id(0) * tt
    first = jnp.logical_or(s0 == 0, s0 == n_lat)
    last = jnp.logical_or(s0 + tt == n_lat, s0 + tt == n_tot)
    zero = jnp.zeros_like(prev_ref[...])
    xe = jnp.concatenate([jnp.where(first, zero, prev_ref[...]), cur_ref[...],
                          jnp.where(last, zero, next_ref[...])], axis=0)
    tc = cur_ref.shape[1]
    rb, cbw = CONV_RB, 512
    half = SSD_CONV // 2
    taps = [k for k in range(SSD_CONV) if k != half]
    wb = w_ref[...].astype(BF16)
    for r0 in range(0, tt, rb):
        win = xe[r0:r0 + rb + 2 * BF16_ROWS, :]
        stacked = jnp.concatenate([win * wb[k:k + 1, :] for k in taps], axis=0)
        sh = _dot(sm_ref[...], stacked)
        for c0 in range(0, tc, cbw):
            acc = (b_ref[:, c0:c0 + cbw] + w_ref[half:half + 1, c0:c0 + cbw] * cur_ref[
                r0:r0 + rb, c0:c0 + cbw].astype(F32)) + sh[:, c0:c0 + cbw]
            y = acc * _sigmoid(acc)
            if rope:
                cosv = cos_ref[r0:r0 + rb, :]
                sinv = sin_ref[r0:r0 + rb, :]
                lane = lax.broadcasted_iota(jnp.int32, (rb, LANE), 1)
                low_half = (lane % 64) < 32
                for s in range(cbw // LANE):
                    yb = y[:, s * LANE:(s + 1) * LANE]
                    sw = jnp.where(low_half, pltpu.roll(yb, LANE - 32, 1), pltpu.roll(yb, 32, 1))
                    o_ref[r0:r0 + rb, c0 + s * LANE:c0 + (s + 1) * LANE] = (
                        yb * cosv + sw * sinv).astype(o_ref.dtype)
            else:
                o_ref[r0:r0 + rb, c0:c0 + cbw] = y.astype(o_ref.dtype)


def _conv_call(proj, shift, conv_w, conv_b, layer, col0, ncols, cos_t, sin_t, *, n_lat, rope, name):
    n_tot = proj.shape[0]
    tt, tc = 256, min(ncols, 4096)
    assert n_lat % tt == 0 and n_tot % tt == 0 and col0 % tc == 0 and ncols % tc == 0
    assert (PROJ_XBC + col0) % tc == 0
    cb = col0 // tc
    pb = (PROJ_XBC + col0) // tc
    hb = tt // BF16_ROWS
    nhb = n_tot // BF16_ROWS
    kern = functools.partial(_conv_kernel, tt=tt, n_lat=n_lat, n_tot=n_tot, rope=rope)
    in_specs = [pl.BlockSpec((BF16_ROWS, tc), lambda i, j: (jnp.maximum(i * hb - 1, 0), pb + j)),
                pl.BlockSpec((tt, tc), lambda i, j: (i, pb + j)),
                pl.BlockSpec((BF16_ROWS, tc), lambda i, j: (jnp.minimum((i + 1) * hb, nhb - 1), pb + j)),
                pl.BlockSpec(shift.shape, lambda i, j: (0, 0)),
                pl.BlockSpec((None, SSD_CONV, tc), lambda i, j: (layer, 0, cb + j)),
                pl.BlockSpec((None, 1, tc), lambda i, j: (layer, 0, cb + j))]
    args = [proj, proj, proj, shift, conv_w, conv_b.reshape(conv_b.shape[0], 1, -1)]
    if rope:
        in_specs += [pl.BlockSpec((tt, LANE), lambda i, j: (i, 0)),
                     pl.BlockSpec((tt, LANE), lambda i, j: (i, 0))]
        args += [cos_t, sin_t]
    return pl.pallas_call(
        kern,
        grid=(n_tot // tt, ncols // tc),
        in_specs=in_specs,
        out_specs=pl.BlockSpec((tt, tc), lambda i, j: (i, j)),
        out_shape=jax.ShapeDtypeStruct((n_tot, ncols), BF16),
        compiler_params=_cparams(2),
        name=name,
    )(*args)


def _conv_shift_matrix():
    half = SSD_CONV // 2
    offs = [k - half for k in range(SSD_CONV) if k != half]
    win = CONV_RB + 2 * BF16_ROWS
    m = np.zeros((CONV_RB, len(offs) * win), np.float32)
    for t, o in enumerate(offs):
        m[np.arange(CONV_RB), t * win + BF16_ROWS + np.arange(CONV_RB) + o] = 1.0
    return jnp.asarray(m, BF16)


def _dt_kernel(raw_ref, bias_ref, alog_ref, cs_ref, w_ref, ecs_ref, cst_ref):
    lane = lax.broadcasted_iota(jnp.int32, (LANE, SSD_CHUNK), 1)
    row = lax.broadcasted_iota(jnp.int32, (LANE, SSD_CHUNK), 0)
    col = lax.broadcasted_iota(jnp.int32, (1, LANE), 1)
    for c in range(DT_CHUNKS):
        tok = slice(c * SSD_CHUNK, (c + 1) * SSD_CHUNK)
        raw = raw_ref[tok, :] + bias_ref[...]
        dt = jnp.maximum(raw, 0.0) + jnp.log1p(jnp.exp(-jnp.abs(raw)))
        da_t = (dt * (-jnp.exp(alog_ref[...]))).T
        pre = da_t
        suf = da_t
        k = 1
        while k < SSD_CHUNK:
            pre = pre + jnp.where(lane >= k, pltpu.roll(pre, k, 1), 0.0)
            suf = suf + jnp.where(lane < SSD_CHUNK - k, pltpu.roll(suf, SSD_CHUNK - k, 1), 0.0)
            k *= 2
        cst = jnp.where(row < SSD_HEADS, pre, suf)
        cs = cst.T
        end = jnp.where(col < SSD_HEADS, cs[SSD_CHUNK - 1:SSD_CHUNK, :], cs[0:1, :])
        cs_ref[tok, :] = cs * LOG2E
        w_ref[tok, :] = jnp.exp(end - cs) * dt
        ecs_ref[tok, :] = jnp.exp(cs)
        cst_ref[:, tok] = cst * LOG2E - jnp.log2(dt).T


DT_CHUNKS = 2


def _dt_call(dt_raw, dt_bias, a_log, layer):
    n_tot = dt_raw.shape[0]
    span = DT_CHUNKS * SSD_CHUNK
    assert n_tot % span == 0
    row_major = pl.BlockSpec((span, LANE), lambda c: (c, 0))
    head_major = pl.BlockSpec((LANE, span), lambda c: (0, c))
    par = pl.BlockSpec((None, 1, LANE), lambda c: (layer, 0, 0))
    return pl.pallas_call(
        _dt_kernel,
        grid=(n_tot // span,),
        in_specs=[row_major, par, par],
        out_specs=[row_major, row_major, row_major, head_major],
        out_shape=[jax.ShapeDtypeStruct((n_tot, LANE), F32)] * 3
        + [jax.ShapeDtypeStruct((LANE, n_tot), F32)],
        compiler_params=_cparams(1),
        name="ssd_dt",
    )(dt_raw, dt_bias.reshape(-1, 1, LANE), a_log.reshape(-1, 1, LANE))


def _split_cat(v, parts):
    out = []
    rem = v
    for _ in range(parts):
        p = rem.astype(BF16)
        out.append(p)
        rem = rem - p.astype(F32)
    return jnp.concatenate(out, axis=1)


def _ssd_kernel(*refs, bwd):
    if bwd:
        (x_ref, bc_ref, cst_ref, cs_ref, w_ref, ecs_ref, sel_ref,
         yf_ref, z_ref, dsk_ref, nw_ref, o_ref, h_ref) = refs
    else:
        (x_ref, bc_ref, cst_ref, cs_ref, w_ref, ecs_ref, sel_ref, o_ref, h_ref) = refs
        yf_ref = z_ref = dsk_ref = nw_ref = None

    @pl.when(pl.program_id(0) == 0)
    def _():
        h_ref[...] = jnp.zeros_like(h_ref)

    subs = range(SSD_CPS - 1, -1, -1) if bwd else range(SSD_CPS)
    for sub in subs:
        _ssd_chunk(slice(sub * SSD_CHUNK, (sub + 1) * SSD_CHUNK), bwd, x_ref, bc_ref, cst_ref, cs_ref,
                   w_ref, ecs_ref, sel_ref, yf_ref, z_ref, dsk_ref, nw_ref, o_ref, h_ref)


def _ssd_chunk(rs, bwd, x_ref, bc_ref, cst_ref, cs_ref, w_ref, ecs_ref, sel_ref, yf_ref, z_ref,
               dsk_ref, nw_ref, o_ref, h_ref):
    w_sp = _split_cat(w_ref[rs, :], 2)
    e_sp = _split_cat(ecs_ref[rs, :], 2)
    li = lax.broadcasted_iota(jnp.int32, (SSD_CHUNK, SSD_CHUNK), 0)
    si = lax.broadcasted_iota(jnp.int32, (SSD_CHUNK, SSD_CHUNK), 1)
    mask = (li <= si) if bwd else (li >= si)
    lane = lax.broadcasted_iota(jnp.int32, (SSD_CHUNK, LANE), 1)
    exit_row = 0 if bwd else SSD_CHUNK - 1
    dir_col = SSD_HEADS if bwd else 0

    for gi in range(SSD_GROUPS):
        gsl = slice(gi * GROUP_W, (gi + 1) * GROUP_W)
        x = x_ref[rs, gsl]
        bm = bc_ref[rs, gi * SSD_STATE:(gi + 1) * SSD_STATE]
        cm = bc_ref[rs, (SSD_GROUPS + gi) * SSD_STATE:(SSD_GROUPS + gi + 1) * SSD_STATE]
        sel = sel_ref[gi]
        w_x = _dot(w_sp, jnp.concatenate([sel, sel], axis=0))
        e_x = _dot(e_sp, jnp.concatenate([sel, sel], axis=0))
        col0 = dir_col + gi * SSD_HPG
        cs8 = cs_ref[rs, col0:col0 + SSD_HPG]

        cb = _dot_nt(cm, bm)
        h = h_ref[gi]
        y_off = _dot(cm, h.astype(BF16)) * e_x
        xw = (x.astype(F32) * w_x).astype(BF16)
        bt = bm.astype(F32).T.astype(BF16)
        h_ref[gi] = h * e_x[exit_row:exit_row + 1, :] + _dot(bt, xw)

        cst = cst_ref[gi * SSD_HPG:(gi + 1) * SSD_HPG, rs]
        ys = []
        for j in range(SSD_HPG // 2):
            ms = []
            for e in (2 * j, 2 * j + 1):
                diff = cs8[:, e:e + 1] - cst[e:e + 1, :]
                m = jnp.where(mask, jnp.exp2(diff), 0.0) * cb
                ms.append(m.astype(BF16))
            xp = x[:, j * LANE:(j + 1) * LANE]
            zero = jnp.zeros_like(xp)
            x2 = jnp.concatenate([jnp.where(lane < SSD_HEADDIM, xp, zero),
                                  jnp.where(lane >= SSD_HEADDIM, xp, zero)], axis=0)
            ys.append(_dot(jnp.concatenate(ms, axis=1), x2))
        y = y_off + jnp.concatenate(ys, axis=1)

        if bwd:
            yt = y + yf_ref[rs, gsl] + dsk_ref[:, gsl] * x.astype(F32)
            zz = z_ref[rs, gsl].astype(F32)
            g = yt * (zz * _sigmoid(zz))
            gs = g * lax.rsqrt(jnp.mean(g * g, axis=-1, keepdims=True) + EPS)
            o_ref[rs, gsl] = (gs * nw_ref[:, gsl]).astype(o_ref.dtype)
        else:
            o_ref[rs, gsl] = y


SSD_CPS = 2


def _ssd_call(xs, bc, dtp, sel, *, n_lat, bwd, yf=None, z=None, dsk=None, nw=None):
    cs, w, ecs, cst = dtp
    n_tot = xs.shape[0]
    span = SSD_CPS * SSD_CHUNK
    assert n_lat % span == 0 and n_tot % span == 0
    ncl = n_lat // span
    nch = n_tot // span
    ncc = nch - ncl
    d = 1 if bwd else 0
    if bwd:
        def cid(s):
            return nch - 1 - s
    else:
        def cid(s):
            return jnp.where(s < ncc, ncl + s, s - ncc)
    tile = pl.BlockSpec((span, D_INNER), lambda s: (cid(s), 0))
    full = pl.BlockSpec((span, LANE), lambda s: (cid(s), 0))
    heads = pl.BlockSpec((SSD_HEADS, span), lambda s: (d, cid(s)))
    in_specs = [tile,
                pl.BlockSpec((span, BC_DIM), lambda s: (cid(s), 0)),
                heads, full, full, full,
                pl.BlockSpec((None, SSD_GROUPS, LANE, GROUP_W), lambda s: (d, 0, 0, 0))]
    args = [xs, bc, cst, cs, w, ecs, sel]
    if bwd:
        in_specs += [tile, tile,
                     pl.BlockSpec((1, D_INNER), lambda s: (0, 0)),
                     pl.BlockSpec((1, D_INNER), lambda s: (0, 0))]
        args += [yf, z, dsk, nw]
    return pl.pallas_call(
        functools.partial(_ssd_kernel, bwd=bwd),
        grid=(nch,),
        in_specs=in_specs,
        out_specs=tile,
        out_shape=jax.ShapeDtypeStruct((n_tot, D_INNER), BF16 if bwd else F32),
        scratch_shapes=[pltpu.VMEM((SSD_GROUPS, SSD_STATE, GROUP_W), F32)],
        compiler_params=_cparams(1),
        name="ssd_bwd" if bwd else "ssd_fwd",
    )(*args)


NA_QROWS = 4
NA_KROWS = NA_QROWS + NA_WIN_H
NA_TOK_BLK = 128


def _na_fill_bias(pp_ref, bias_ref):
    lane = lax.broadcasted_iota(jnp.int32, (GRID_W, 2 * GRID_W), 1)
    ninf = jnp.full((GRID_W, 2 * GRID_W), -jnp.inf, F32)
    for v in range(3):
        for kk in range(NA_KROWS):
            for p in range(NA_QROWS // 2):
                i0 = 2 * p
                if v == 1:
                    ok0, ok1 = 0 <= kk - i0 < NA_WIN_H, 0 <= kk - i0 - 1 < NA_WIN_H
                    d0 = kk - i0 + NA_WIN_H // 2 - 1
                else:
                    ok0 = ok1 = kk < NA_WIN_H
                    d0 = kk - i0 + (NA_WIN_H - 1 if v == 0 else NA_WIN_H // 2 - 1)
                if ok0 and ok1:
                    blk = pp_ref[d0]
                elif ok0:
                    blk = jnp.where(lane < GRID_W, pp_ref[d0], ninf)
                elif ok1:
                    blk = jnp.where(lane >= GRID_W, pp_ref[d0], ninf)
                else:
                    blk = ninf
                bias_ref[v, kk * GRID_W:(kk + 1) * GRID_W, p * 2 * GRID_W:(p + 1) * 2 * GRID_W] = blk


def _na_kernel(q_ref, k_ref, v_ref, pp_ref, o_ref, qt_ref, vt_ref, bias_ref, *, n_lat, n_ctx):
    _na_fill_bias(pp_ref, bias_ref)
    rows = n_lat // GRID_W
    qscale = NA_HEAD_DIM ** -0.5 * LOG2E
    nq = NA_QROWS * GRID_W
    nkeys = NA_KROWS * GRID_W
    qblk = nq // NA_TOK_BLK
    kblk = nkeys // NA_TOK_BLK
    lat_blk = n_lat // NA_TOK_BLK
    ctx_blk = n_ctx // NA_TOK_BLK

    def transpose_blocks(p, carry):
        t0 = pl.multiple_of(p * NA_TOK_BLK, NA_TOK_BLK)
        qt_ref[p] = (q_ref[pl.ds(t0, NA_TOK_BLK), :].astype(F32) * qscale).astype(BF16).T
        vt_ref[p] = v_ref[pl.ds(t0, NA_TOK_BLK), :].T
        return carry

    lax.fori_loop(0, lat_blk + ctx_blk, transpose_blocks, 0, unroll=6)

    kc = k_ref[n_lat:n_lat + n_ctx, :]
    vtc = jnp.concatenate([vt_ref[lat_blk + b] for b in range(ctx_blk)], axis=1)

    def softmax_pv(parts):
        m = parts[0][0].max(axis=0, keepdims=True)
        for s, _ in parts[1:]:
            m = jnp.maximum(m, s.max(axis=0, keepdims=True))
        den = 0.0
        acc = 0.0
        for s, vt in parts:
            p = jnp.exp2(s - m)
            den = den + p.sum(axis=0, keepdims=True)
            acc = acc + _dot(vt, p.astype(BF16))
        return acc / den

    def body(t, carry):
        r0 = t * NA_QROWS
        ws = jnp.clip(r0 - NA_WIN_H // 2, 0, rows - NA_WIN_H)
        variant = jnp.where(r0 == 0, 0, jnp.where(r0 == rows - NA_QROWS, 2, 1))
        q0 = pl.multiple_of(r0 * GRID_W, nq)
        k0 = pl.multiple_of(ws * GRID_W, nq)
        kb0 = ws * GRID_W // NA_TOK_BLK
        qt = jnp.concatenate([qt_ref[t * qblk + b] for b in range(qblk)], axis=1)
        vt = jnp.concatenate([vt_ref[kb0 + b] for b in range(kblk)], axis=1)
        s_loc = _dot(k_ref[pl.ds(k0, nkeys), :], qt) + bias_ref[variant]
        s_ctx = _dot(kc, qt)
        ot = softmax_pv([(s_loc, vt), (s_ctx, vtc)])
        o_ref[pl.ds(q0, nq), :] = ot.T.astype(o_ref.dtype)
        return carry

    lax.fori_loop(0, rows // NA_QROWS, body, 0, unroll=8)
    qtc = jnp.concatenate([qt_ref[lat_blk + b] for b in range(ctx_blk)], axis=1)
    otc = softmax_pv([(_dot(kc, qtc), vtc)])
    o_ref[n_lat:n_lat + n_ctx, :] = otc.T.astype(o_ref.dtype)


def _na_call(proj, pairs, *, n_lat):
    n_tot = proj.shape[0]
    n_ctx = n_tot - n_lat
    nblk = n_tot // NA_TOK_BLK
    assert n_ctx >= NA_QROWS * GRID_W
    q0 = PROJ_QKVG // NA_HEAD_DIM

    def col(base):
        return pl.BlockSpec((n_tot, NA_HEAD_DIM), lambda h: (0, base + h))
    return pl.pallas_call(
        functools.partial(_na_kernel, n_lat=n_lat, n_ctx=n_ctx),
        grid=(NA_HEADS,),
        in_specs=[col(q0), col(q0 + NA_HEADS), col(q0 + 2 * NA_HEADS),
                  pl.BlockSpec((None, 2 * NA_WIN_H - 1, GRID_W, 2 * GRID_W), lambda h: (h, 0, 0, 0))],
        out_specs=col(0),
        out_shape=jax.ShapeDtypeStruct((n_tot, D_MODEL), BF16),
        scratch_shapes=[pltpu.VMEM((nblk, NA_HEAD_DIM, NA_TOK_BLK), BF16),
                        pltpu.VMEM((nblk, NA_HEAD_DIM, NA_TOK_BLK), BF16),
                        pltpu.VMEM((3, NA_KROWS * GRID_W, NA_QROWS * GRID_W), F32)],
        compiler_params=_cparams(1),
        name="na_attn",
    )(proj, proj, proj, pairs)


def _na_bias_pairs(rpb):
    c = np.arange(GRID_W)[:, None]
    kc = np.arange(GRID_W)[None, :]
    cstart = np.clip(c - NA_WIN_W // 2, 0, GRID_W - NA_WIN_W)
    col_ok = (kc >= cstart) & (kc < cstart + NA_WIN_W)
    dcol = np.clip(kc - c, -(NA_WIN_W - 1), NA_WIN_W - 1) + NA_WIN_W - 1
    onehot = (dcol.T[None] == np.arange(2 * NA_WIN_W - 1)[:, None, None]).astype(np.float32)
    a = jnp.einsum("hdj,jkc->hdkc", rpb, jnp.asarray(onehot), precision=lax.Precision.HIGHEST)
    a = jnp.where(col_ok.T[None, None], a * LOG2E, -jnp.inf).astype(F32)
    prev = jnp.concatenate([jnp.full_like(a[:, :1], -jnp.inf), a[:, :-1]], axis=1)
    return jnp.concatenate([a, prev], axis=-1)


def _gate_kernel(gn_ref, o_ref_in, ga_ref, gb_ref, w1_ref, w2_ref, out_ref):
    y1 = _dot(gn_ref[...], w1_ref[...].astype(BF16))
    y2 = _dot(o_ref_in[...], w2_ref[...].astype(BF16))
    out = _sigmoid(ga_ref[...].astype(F32)) * y1 + _sigmoid(gb_ref[...].astype(F32)) * y2
    out_ref[...] = out.astype(out_ref.dtype)


def _gate_call(gn, o_na, proj, w_ssd_o, w_na_o, layer, *, rows):
    tm = _pick_tile(rows, 1408)
    tn = 256
    ga0 = (PROJ_QKVG + 3 * D_MODEL) // tn
    gb0 = (PROJ_QKVG + 4 * D_MODEL) // tn
    return pl.pallas_call(
        _gate_kernel,
        grid=(rows // tm, D_MODEL // tn),
        in_specs=[pl.BlockSpec((tm, D_INNER), lambda i, j: (i, 0)),
                  pl.BlockSpec((tm, D_MODEL), lambda i, j: (i, 0)),
                  pl.BlockSpec((tm, tn), lambda i, j: (i, ga0 + j)),
                  pl.BlockSpec((tm, tn), lambda i, j: (i, gb0 + j)),
                  pl.BlockSpec((None, D_INNER, tn), lambda i, j: (layer, 0, j)),
                  pl.BlockSpec((None, D_MODEL, tn), lambda i, j: (layer, 0, j))],
        out_specs=pl.BlockSpec((tm, tn), lambda i, j: (i, j)),
        out_shape=jax.ShapeDtypeStruct((rows, D_MODEL), BF16),
        compiler_params=_cparams(2),
        name="gate_merge",
    )(gn, o_na, proj, proj, w_ssd_o, w_na_o)


def _rope_tables(n_lat, n_ctx):
    t = jnp.arange(n_lat, dtype=jnp.int32)
    n_ax = SSD_STATE // 2
    inv = ROPE_BASE ** (-jnp.arange(0, n_ax, 2, dtype=F32) / n_ax)
    ang_r = (t // GRID_W).astype(F32)[:, None] * inv
    ang_c = (t % GRID_W).astype(F32)[:, None] * inv
    cos_l = jnp.concatenate([jnp.cos(ang_r)] * 2 + [jnp.cos(ang_c)] * 2, axis=1)
    sin_l = jnp.concatenate([-jnp.sin(ang_r), jnp.sin(ang_r), -jnp.sin(ang_c), jnp.sin(ang_c)], axis=1)
    cos_t = jnp.concatenate([cos_l, jnp.ones((n_ctx, LANE), F32)], axis=0)
    sin_t = jnp.concatenate([sin_l, jnp.zeros((n_ctx, LANE), F32)], axis=0)
    return cos_t, sin_t


def _head_selectors():
    sel = np.zeros((2, SSD_GROUPS, LANE, GROUP_W), np.float32)
    for d in range(2):
        for g in range(SSD_GROUPS):
            for e in range(SSD_HPG):
                hd = d * SSD_HEADS + g * SSD_HPG + e
                sel[d, g, hd, e * SSD_HEADDIM:(e + 1) * SSD_HEADDIM] = 1.0
    return jnp.asarray(sel, BF16)


def kernel(x, c, ctx, c_ctx, w_ada, b_ada, g_pre_mix, g_post_mix, g_pre_mlp, g_post_mlp, w_in,
           conv_w, conv_b, a_log, dt_bias, d_skip, ssd_norm, w_ssd_o, rpb, w_na_o, w_out,
           w_mlp1, w_mlp2):
    assert x.shape[0] == 1 and x.shape[2] == D_MODEL
    depth = w_ada.shape[0]
    n_lat = x.shape[1]
    n_ctx = ctx.shape[1]
    n_tot = n_lat + n_ctx
    assert n_lat % (GRID_W * NA_WIN_H) == 0 and n_ctx % 256 == 0 and n_lat % 256 == 0

    cc = jnp.zeros((SUBLANE, D_MODEL), F32).at[0].set(c[0]).at[1].set(c_ctx)
    mod = _ada_call(cc, w_ada, b_ada)
    cos_t, sin_t = _rope_tables(n_lat, n_ctx)
    sel = _head_selectors()
    shift = _conv_shift_matrix()
    w_out_b = w_out.astype(BF16)
    w_mlp2_b = w_mlp2.astype(BF16)
    dsk_all = jnp.repeat(d_skip[:, 0] + d_skip[:, 1], SSD_HEADDIM, axis=-1)

    def row(v):
        return v.reshape(1, -1)

    stream, u = _stream_call(x.reshape(n_lat, D_MODEL), ctx.reshape(n_ctx, D_MODEL),
                             row(g_pre_mix[0]), mod)
    for l in range(depth):
        last = l == depth - 1
        rows_out = n_lat if last else n_tot
        proj, dt_raw = _inproj_call(u, w_in, l)

        xs = _conv_call(proj, shift, conv_w, conv_b, l, 0, D_INNER, None, None, n_lat=n_lat,
                        rope=False, name="conv_x")
        bc = _conv_call(proj, shift, conv_w, conv_b, l, D_INNER, BC_DIM, cos_t, sin_t, n_lat=n_lat,
                        rope=True, name="conv_bc")
        dtp = _dt_call(dt_raw, dt_bias, a_log, l)
        y_f = _ssd_call(xs, bc, dtp, sel, n_lat=n_lat, bwd=False)
        gn = _ssd_call(xs, bc, dtp, sel, n_lat=n_lat, bwd=True, yf=y_f, z=proj,
                       dsk=row(dsk_all[l]), nw=row(ssd_norm[l]))

        o_na = _na_call(proj, _na_bias_pairs(rpb[l]), n_lat=n_lat)
        merged = _gate_call(gn, o_na, proj, w_ssd_o, w_na_o, l, rows=rows_out)
        stream_mid, h = _mm_res_call(merged, w_out_b, stream, row(g_post_mix[l]), mod, l, 2,
                                     1024, rows=rows_out, n_lat=n_lat, name="out_proj",
                                     nxt=(row(g_pre_mlp[l]), l, 3, 4))
        hidden = _mlp_up_call(h, w_mlp1, l, rows=rows_out)
        nxt = None if last else (row(g_pre_mix[l + 1]), l + 1, 0, 1)
        res = _mm_res_call(hidden, w_mlp2_b, stream_mid, row(g_post_mlp[l]), mod, l, 5, 256,
                           rows=rows_out, n_lat=n_lat, name="mlp_down", nxt=nxt)
        stream, u = (res, None) if last else res
    return stream[None]
```

```python
import functools

import numpy as np
import jax
import jax.numpy as jnp
from jax import lax
from jax.experimental import pallas as pl
from jax.experimental.pallas import tpu as pltpu

F32 = jnp.float32
BF16 = jnp.bfloat16

EPS = 1e-6
LOG2E = 1.4426950408889634
ROPE_BASE = 10000.0
GRID_W = 64
D_MODEL = 2048
D_INNER = 2 * D_MODEL
SSD_HEADDIM = 64
SSD_HEADS = D_INNER // SSD_HEADDIM
SSD_GROUPS = 8
SSD_HPG = SSD_HEADS // SSD_GROUPS
SSD_STATE = 128
SSD_CONV = 5
SSD_CHUNK = 128
GROUP_W = SSD_HPG * SSD_HEADDIM
BC_DIM = 2 * SSD_GROUPS * SSD_STATE
CONV_DIM = D_INNER + BC_DIM
NA_HEAD_DIM = 128
NA_HEADS = D_MODEL // NA_HEAD_DIM
NA_WIN_H = 8
NA_WIN_W = 16
D_FF = 4 * D_MODEL
OFF_Z = 0
OFF_XBC = D_INNER
OFF_DT = OFF_XBC + CONV_DIM
OFF_QKVG = OFF_DT + 2 * SSD_HEADS
QKVG_DIM = 5 * D_MODEL
PROJ_Z = 0
PROJ_XBC = D_INNER
PROJ_QKVG = PROJ_XBC + CONV_DIM
PROJ_DIM = PROJ_QKVG + QKVG_DIM

LANE = 128
SUBLANE = 8
BF16_ROWS = 16
VMEM_LIMIT_MB = 56


def _cparams(n_axes, vmem_mb=VMEM_LIMIT_MB):
    return pltpu.CompilerParams(dimension_semantics=("arbitrary",) * n_axes,
                                vmem_limit_bytes=vmem_mb * 1024 * 1024)


def _pick_tile(rows, cap):
    best = None
    for d in range(BF16_ROWS, cap + 1, BF16_ROWS):
        if rows % d == 0:
            best = d
    assert best is not None, (rows, cap)
    return best


EPILOGUE_ROWS = BF16_ROWS


def _sigmoid(v):
    return 1.0 / (1.0 + jnp.exp2(v * -LOG2E))


def _dot(a, b):
    return jnp.dot(a, b, preferred_element_type=F32)


def _dot_nt(a, b):
    return lax.dot_general(a, b, (((1,), (1,)), ((), ())), preferred_element_type=F32)


def _select_rows(ref, row0, rc, n_lat):
    rows = row0 + lax.broadcasted_iota(jnp.int32, (rc, 1), 0)
    return jnp.where(rows >= n_lat, ref[1:2, :], ref[0:1, :])


def _ada_kernel(c_ref, w_ref, b_ref, o_ref):
    cv = c_ref[...]
    s = cv * _sigmoid(cv)
    o_ref[...] = _dot(s.astype(BF16), w_ref[...].astype(BF16)) + b_ref[...]


def _ada_call(cc, w_ada, b_ada):
    depth, d, n6 = w_ada.shape
    tn = 1024
    return pl.pallas_call(
        _ada_kernel,
        grid=(depth, n6 // tn),
        in_specs=[pl.BlockSpec((SUBLANE, d), lambda l, j: (0, 0)),
                  pl.BlockSpec((None, d, tn), lambda l, j: (l, 0, j)),
                  pl.BlockSpec((None, 1, tn), lambda l, j: (l, 0, j))],
        out_specs=pl.BlockSpec((None, SUBLANE, tn), lambda l, j: (l, 0, j)),
        out_shape=jax.ShapeDtypeStruct((depth, SUBLANE, n6), F32),
        compiler_params=_cparams(2),
        name="ada",
    )(cc, w_ada, b_ada.reshape(depth, 1, n6))


def _norm_modulate(v, g_ref, sh, sc):
    ms = jnp.mean(v * v, axis=-1, keepdims=True)
    return v * lax.rsqrt(ms + EPS) * g_ref[...] * (1.0 + sc) + sh


def _stream_kernel(x_ref, c_ref, g_ref, sh_ref, sc_ref, s_ref, u_ref, *, lat_tiles):
    i = pl.program_id(0)

    def emit(v, row):
        s_ref[...] = v
        u_ref[...] = _norm_modulate(v, g_ref, sh_ref[row:row + 1, :], sc_ref[row:row + 1, :]
                                    ).astype(u_ref.dtype)

    @pl.when(i < lat_tiles)
    def _():
        emit(x_ref[...], 0)

    @pl.when(i >= lat_tiles)
    def _():
        emit(c_ref[...], 1)


def _stream_call(x2, ctx2, g, mod):
    n_lat, d = x2.shape
    n_ctx = ctx2.shape[0]
    tr = 256
    lat_tiles = n_lat // tr
    row = pl.BlockSpec((tr, d), lambda i: (i, 0))
    return pl.pallas_call(
        functools.partial(_stream_kernel, lat_tiles=lat_tiles),
        grid=((n_lat + n_ctx) // tr,),
        in_specs=[pl.BlockSpec((tr, d), lambda i: (jnp.minimum(i, lat_tiles - 1), 0)),
                  pl.BlockSpec((tr, d), lambda i: (jnp.maximum(i - lat_tiles, 0), 0)),
                  pl.BlockSpec((1, d), lambda i: (0, 0)),
                  pl.BlockSpec((None, SUBLANE, d), lambda i: (0, 0, 0)),
                  pl.BlockSpec((None, SUBLANE, d), lambda i: (0, 0, 1))],
        out_specs=[row, row],
        out_shape=[jax.ShapeDtypeStruct((n_lat + n_ctx, d), F32),
                   jax.ShapeDtypeStruct((n_lat + n_ctx, d), BF16)],
        compiler_params=_cparams(1),
        name="stream",
    )(x2, ctx2, g, mod, mod)


def _inproj_kernel(a_ref, w_ref, wdt_ref, p_ref, dt_ref):
    a = a_ref[...]
    p_ref[...] = _dot(a, w_ref[0].astype(BF16)).astype(p_ref.dtype)

    @pl.when(pl.program_id(1) == 0)
    def _():
        dt_ref[...] = _dot(a, wdt_ref[...].astype(BF16))


def _inproj_call(u, w_in, layer):
    rows, d = u.shape
    tm = _pick_tile(rows, 2112)
    tn = 512
    dtw = 2 * SSD_HEADS

    def wcol(i, j):
        return (layer, 0, pl.multiple_of(j * tn + jnp.where(j >= PROJ_QKVG // tn, dtw, 0), dtw))
    return pl.pallas_call(
        _inproj_kernel,
        grid=(rows // tm, PROJ_DIM // tn),
        in_specs=[pl.BlockSpec((tm, d), lambda i, j: (i, 0)),
                  pl.BlockSpec((pl.Element(1), pl.Element(d), pl.Element(tn)), wcol),
                  pl.BlockSpec((None, d, dtw), lambda i, j: (layer, 0, OFF_DT // dtw))],
        out_specs=[pl.BlockSpec((tm, tn), lambda i, j: (i, j)),
                   pl.BlockSpec((tm, dtw), lambda i, j: (i, 0))],
        out_shape=[jax.ShapeDtypeStruct((rows, PROJ_DIM), BF16),
                   jax.ShapeDtypeStruct((rows, dtw), F32)],
        compiler_params=_cparams(2),
        name="inproj",
    )(u, w_in, w_in)


def _mlp_up_kernel(a_ref, w_ref, o_ref):
    acc = _dot(a_ref[...], w_ref[...].astype(BF16))
    o_ref[...] = jnp.square(jnp.maximum(acc, 0.0)).astype(o_ref.dtype)


def _mlp_up_call(h, w1, layer, *, rows):
    d = h.shape[1]
    dff = w1.shape[2]
    tm = _pick_tile(rows, 2112)
    tn = 512
    return pl.pallas_call(
        _mlp_up_kernel,
        grid=(rows // tm, dff // tn),
        in_specs=[pl.BlockSpec((tm, d), lambda i, j: (i, 0)),
                  pl.BlockSpec((None, d, tn), lambda i, j: (layer, 0, j))],
        out_specs=pl.BlockSpec((tm, tn), lambda i, j: (i, j)),
        out_shape=jax.ShapeDtypeStruct((rows, dff), BF16),
        compiler_params=_cparams(2),
        name="mlp_up",
    )(h, w1)


def _mm_res_kernel(*refs, tm, nj, n_lat, emit_next, all_latent):
    if emit_next:
        (a_ref, w_ref, x_ref, g_ref, gt_ref, g2_ref, sh_ref, sc_ref, o_ref, u_ref, acc_ref) = refs
    else:
        (a_ref, w_ref, x_ref, g_ref, gt_ref, o_ref, acc_ref) = refs
    i = pl.program_id(0)
    j = pl.program_id(1)
    acc_ref[j] = _dot(a_ref[...], w_ref[...])
    rc = EPILOGUE_ROWS
    tile_lo = i * tm

    def epilogue(pick):
        def body(q, carry):
            r0 = pl.multiple_of(q * rc, rc)
            y = jnp.concatenate([acc_ref[jj, pl.ds(r0, rc), :] for jj in range(nj)], axis=1)
            ms = jnp.mean(y * y, axis=-1, keepdims=True)
            r = y * lax.rsqrt(ms + EPS) * g_ref[...]
            xn = x_ref[pl.ds(r0, rc), :] + pick(gt_ref, r0) * r
            o_ref[pl.ds(r0, rc), :] = xn
            if emit_next:
                u_ref[pl.ds(r0, rc), :] = _norm_modulate(
                    xn, g2_ref, pick(sh_ref, r0), pick(sc_ref, r0)).astype(u_ref.dtype)
            return carry
        lax.fori_loop(0, tm // rc, body, 0, unroll=3 if (tm // rc) % 3 == 0 else 2)

    last = j == nj - 1
    if all_latent:
        pl.when(last)(lambda: epilogue(lambda ref, r0: ref[0:1, :]))
    else:
        side = (tile_lo >= n_lat).astype(jnp.int32)
        mixed = jnp.logical_and(tile_lo < n_lat, tile_lo + tm > n_lat)
        pl.when(jnp.logical_and(last, jnp.logical_not(mixed)))(
            lambda: epilogue(lambda ref, r0: ref[pl.ds(side, 1), :]))
        pl.when(jnp.logical_and(last, mixed))(
            lambda: epilogue(lambda ref, r0: _select_rows(ref, tile_lo + r0, rc, n_lat)))


def _mm_res_call(a, w_bf16, x, g, mod, layer, gt_idx, tn, *, rows, n_lat, name, nxt=None):
    _, kdim, d = w_bf16.shape
    tm = _pick_tile(rows, 528)
    nj = d // tn
    emit_next = nxt is not None
    kern = functools.partial(_mm_res_kernel, tm=tm, nj=nj, n_lat=n_lat, emit_next=emit_next,
                             all_latent=rows <= n_lat)
    in_specs = [pl.BlockSpec((tm, kdim), lambda i, k: (i, 0)),
                pl.BlockSpec((None, kdim, tn), lambda i, k: (layer, 0, k)),
                pl.BlockSpec((tm, d), lambda i, k: (i, 0)),
                pl.BlockSpec((1, d), lambda i, k: (0, 0)),
                pl.BlockSpec((None, SUBLANE, d), lambda i, k: (layer, 0, gt_idx))]
    args = [a, w_bf16, x, g, mod]
    row = pl.BlockSpec((tm, d), lambda i, k: (i, 0))
    out_specs = row
    out_shape = jax.ShapeDtypeStruct((rows, d), F32)
    if emit_next:
        g2, layer2, sh_idx, sc_idx = nxt
        in_specs += [pl.BlockSpec((1, d), lambda i, k: (0, 0)),
                     pl.BlockSpec((None, SUBLANE, d), lambda i, k: (layer2, 0, sh_idx)),
                     pl.BlockSpec((None, SUBLANE, d), lambda i, k: (layer2, 0, sc_idx))]
        args += [g2, mod, mod]
        out_specs = [row, row]
        out_shape = [out_shape, jax.ShapeDtypeStruct((rows, d), BF16)]
    return pl.pallas_call(
        kern,
        grid=(rows // tm, nj),
        in_specs=in_specs,
        out_specs=out_specs,
        out_shape=out_shape,
        scratch_shapes=[pltpu.VMEM((nj, tm, tn), F32)],
        compiler_params=_cparams(2),
        name=name,
    )(*args)


CONV_RB = 128

def _conv_kernel(*refs, tt, n_lat, n_tot, rope):
    if rope:
        prev_ref, cur_ref, next_ref, sm_ref, w_ref, b_ref, cos_ref, sin_ref, o_ref = refs
    else:
        prev_ref, cur_ref, next_ref, sm_ref, w_ref, b_ref, o_ref = refs
    s0 = pl.program_id(0) * tt
    first = jnp.logical_or(s0 == 0, s0 == n_lat)
    last = jnp.logical_or(s0 + tt == n_lat, s0 + tt == n_tot)
    zero = jnp.zeros_like(prev_ref[...])
    xe = jnp.concatenate([jnp.where(first, zero, prev_ref[...]), cur_ref[...],
                          jnp.where(last, zero, next_ref[...])], axis=0)
    tc = cur_ref.shape[1]
    rb, cbw = CONV_RB, 512
    half = SSD_CONV // 2
    taps = [k for k in range(SSD_CONV) if k != half]
    wb = w_ref[...].astype(BF16)
    for r0 in range(0, tt, rb):
        win = xe[r0:r0 + rb + 2 * BF16_ROWS, :]
        stacked = jnp.concatenate([win * wb[k:k + 1, :] for k in taps], axis=0)
        sh = _dot(sm_ref[...], stacked)
        for c0 in range(0, tc, cbw):
            acc = (b_ref[:, c0:c0 + cbw] + w_ref[half:half + 1, c0:c0 + cbw] * cur_ref[
                r0:r0 + rb, c0:c0 + cbw].astype(F32)) + sh[:, c0:c0 + cbw]
            y = acc * _sigmoid(acc)
            if rope:
                cosv = cos_ref[r0:r0 + rb, :]
                sinv = sin_ref[r0:r0 + rb, :]
                lane = lax.broadcasted_iota(jnp.int32, (rb, LANE), 1)
                low_half = (lane % 64) < 32
                for s in range(cbw // LANE):
                    yb = y[:, s * LANE:(s + 1) * LANE]
                    sw = jnp.where(low_half, pltpu.roll(yb, LANE - 32, 1), pltpu.roll(yb, 32, 1))
                    o_ref[r0:r0 + rb, c0 + s * LANE:c0 + (s + 1) * LANE] = (
                        yb * cosv + sw * sinv).astype(o_ref.dtype)
            else:
                o_ref[r0:r0 + rb, c0:c0 + cbw] = y.astype(o_ref.dtype)


def _conv_call(proj, shift, conv_w, conv_b, layer, col0, ncols, cos_t, sin_t, *, n_lat, rope, name):
    n_tot = proj.shape[0]
    tt, tc = 256, min(ncols, 4096)
    assert n_lat % tt == 0 and n_tot % tt == 0 and col0 % tc == 0 and ncols % tc == 0
    assert (PROJ_XBC + col0) % tc == 0
    cb = col0 // tc
    pb = (PROJ_XBC + col0) // tc
    hb = tt // BF16_ROWS
    nhb = n_tot // BF16_ROWS
    kern = functools.partial(_conv_kernel, tt=tt, n_lat=n_lat, n_tot=n_tot, rope=rope)
    in_specs = [pl.BlockSpec((BF16_ROWS, tc), lambda i, j: (jnp.maximum(i * hb - 1, 0), pb + j)),
                pl.BlockSpec((tt, tc), lambda i, j: (i, pb + j)),
                pl.BlockSpec((BF16_ROWS, tc), lambda i, j: (jnp.minimum((i + 1) * hb, nhb - 1), pb + j)),
                pl.BlockSpec(shift.shape, lambda i, j: (0, 0)),
                pl.BlockSpec((None, SSD_CONV, tc), lambda i, j: (layer, 0, cb + j)),
                pl.BlockSpec((None, 1, tc), lambda i, j: (layer, 0, cb + j))]
    args = [proj, proj, proj, shift, conv_w, conv_b.reshape(conv_b.shape[0], 1, -1)]
    if rope:
        in_specs += [pl.BlockSpec((tt, LANE), lambda i, j: (i, 0)),
                     pl.BlockSpec((tt, LANE), lambda i, j: (i, 0))]
        args += [cos_t, sin_t]
    return pl.pallas_call(
        kern,
        grid=(n_tot // tt, ncols // tc),
        in_specs=in_specs,
        out_specs=pl.BlockSpec((tt, tc), lambda i, j: (i, j)),
        out_shape=jax.ShapeDtypeStruct((n_tot, ncols), BF16),
        compiler_params=_cparams(2),
        name=name,
    )(*args)


def _conv_shift_matrix():
    half = SSD_CONV // 2
    offs = [k - half for k in range(SSD_CONV) if k != half]
    win = CONV_RB + 2 * BF16_ROWS
    m = np.zeros((CONV_RB, len(offs) * win), np.float32)
    for t, o in enumerate(offs):
        m[np.arange(CONV_RB), t * win + BF16_ROWS + np.arange(CONV_RB) + o] = 1.0
    return jnp.asarray(m, BF16)


def _dt_kernel(raw_ref, bias_ref, alog_ref, cs_ref, w_ref, ecs_ref, cst_ref):
    lane = lax.broadcasted_iota(jnp.int32, (LANE, SSD_CHUNK), 1)
    row = lax.broadcasted_iota(jnp.int32, (LANE, SSD_CHUNK), 0)
    col = lax.broadcasted_iota(jnp.int32, (1, LANE), 1)
    for c in range(DT_CHUNKS):
        tok = slice(c * SSD_CHUNK, (c + 1) * SSD_CHUNK)
        raw = raw_ref[tok, :] + bias_ref[...]
        dt = jnp.maximum(raw, 0.0) + jnp.log1p(jnp.exp(-jnp.abs(raw)))
        da_t = (dt * (-jnp.exp(alog_ref[...]))).T
        pre = da_t
        suf = da_t
        k = 1
        while k < SSD_CHUNK:
            pre = pre + jnp.where(lane >= k, pltpu.roll(pre, k, 1), 0.0)
            suf = suf + jnp.where(lane < SSD_CHUNK - k, pltpu.roll(suf, SSD_CHUNK - k, 1), 0.0)
            k *= 2
        cst = jnp.where(row < SSD_HEADS, pre, suf)
        cs = cst.T
        end = jnp.where(col < SSD_HEADS, cs[SSD_CHUNK - 1:SSD_CHUNK, :], cs[0:1, :])
        cs_ref[tok, :] = cs * LOG2E
        w_ref[tok, :] = jnp.exp(end - cs) * dt
        ecs_ref[tok, :] = jnp.exp(cs)
        cst_ref[:, tok] = cst * LOG2E - jnp.log2(dt).T


DT_CHUNKS = 2


def _dt_call(dt_raw, dt_bias, a_log, layer):
    n_tot = dt_raw.shape[0]
    span = DT_CHUNKS * SSD_CHUNK
    assert n_tot % span == 0
    row_major = pl.BlockSpec((span, LANE), lambda c: (c, 0))
    head_major = pl.BlockSpec((LANE, span), lambda c: (0, c))
    par = pl.BlockSpec((None, 1, LANE), lambda c: (layer, 0, 0))
    return pl.pallas_call(
        _dt_kernel,
        grid=(n_tot // span,),
        in_specs=[row_major, par, par],
        out_specs=[row_major, row_major, row_major, head_major],
        out_shape=[jax.ShapeDtypeStruct((n_tot, LANE), F32)] * 3
        + [jax.ShapeDtypeStruct((LANE, n_tot), F32)],
        compiler_params=_cparams(1),
        name="ssd_dt",
    )(dt_raw, dt_bias.reshape(-1, 1, LANE), a_log.reshape(-1, 1, LANE))


def _split_cat(v, parts):
    out = []
    rem = v
    for _ in range(parts):
        p = rem.astype(BF16)
        out.append(p)
        rem = rem - p.astype(F32)
    return jnp.concatenate(out, axis=1)


def _ssd_kernel(*refs, bwd):
    if bwd:
        (x_ref, bc_ref, cst_ref, cs_ref, w_ref, ecs_ref, sel_ref,
         yf_ref, z_ref, dsk_ref, nw_ref, o_ref, h_ref) = refs
    else:
        (x_ref, bc_ref, cst_ref, cs_ref, w_ref, ecs_ref, sel_ref, o_ref, h_ref) = refs
        yf_ref = z_ref = dsk_ref = nw_ref = None

    @pl.when(pl.program_id(0) == 0)
    def _():
        h_ref[...] = jnp.zeros_like(h_ref)

    subs = range(SSD_CPS - 1, -1, -1) if bwd else range(SSD_CPS)
    for sub in subs:
        _ssd_chunk(slice(sub * SSD_CHUNK, (sub + 1) * SSD_CHUNK), bwd, x_ref, bc_ref, cst_ref, cs_ref,
                   w_ref, ecs_ref, sel_ref, yf_ref, z_ref, dsk_ref, nw_ref, o_ref, h_ref)


def _ssd_chunk(rs, bwd, x_ref, bc_ref, cst_ref, cs_ref, w_ref, ecs_ref, sel_ref, yf_ref, z_ref,
               dsk_ref, nw_ref, o_ref, h_ref):
    w_sp = _split_cat(w_ref[rs, :], 2)
    e_sp = _split_cat(ecs_ref[rs, :], 2)
    li = lax.broadcasted_iota(jnp.int32, (SSD_CHUNK, SSD_CHUNK), 0)
    si = lax.broadcasted_iota(jnp.int32, (SSD_CHUNK, SSD_CHUNK), 1)
    mask = (li <= si) if bwd else (li >= si)
    lane = lax.broadcasted_iota(jnp.int32, (SSD_CHUNK, LANE), 1)
    exit_row = 0 if bwd else SSD_CHUNK - 1
    dir_col = SSD_HEADS if bwd else 0

    for gi in range(SSD_GROUPS):
        gsl = slice(gi * GROUP_W, (gi + 1) * GROUP_W)
        x = x_ref[rs, gsl]
        bm = bc_ref[rs, gi * SSD_STATE:(gi + 1) * SSD_STATE]
        cm = bc_ref[rs, (SSD_GROUPS + gi) * SSD_STATE:(SSD_GROUPS + gi + 1) * SSD_STATE]
        sel = sel_ref[gi]
        w_x = _dot(w_sp, jnp.concatenate([sel, sel], axis=0))
        e_x = _dot(e_sp, jnp.concatenate([sel, sel], axis=0))
        col0 = dir_col + gi * SSD_HPG
        cs8 = cs_ref[rs, col0:col0 + SSD_HPG]

        cb = _dot_nt(cm, bm)
        h = h_ref[gi]
        y_off = _dot(cm, h.astype(BF16)) * e_x
        xw = (x.astype(F32) * w_x).astype(BF16)
        bt = bm.astype(F32).T.astype(BF16)
        h_ref[gi] = h * e_x[exit_row:exit_row + 1, :] + _dot(bt, xw)

        cst = cst_ref[gi * SSD_HPG:(gi + 1) * SSD_HPG, rs]
        ys = []
        for j in range(SSD_HPG // 2):
            ms = []
            for e in (2 * j, 2 * j + 1):
                diff = cs8[:, e:e + 1] - cst[e:e + 1, :]
                m = jnp.where(mask, jnp.exp2(diff), 0.0) * cb
                ms.append(m.astype(BF16))
            xp = x[:, j * LANE:(j + 1) * LANE]
            zero = jnp.zeros_like(xp)
            x2 = jnp.concatenate([jnp.where(lane < SSD_HEADDIM, xp, zero),
                                  jnp.where(lane >= SSD_HEADDIM, xp, zero)], axis=0)
            ys.append(_dot(jnp.concatenate(ms, axis=1), x2))
        y = y_off + jnp.concatenate(ys, axis=1)

        if bwd:
            yt = y + yf_ref[rs, gsl] + dsk_ref[:, gsl] * x.astype(F32)
            zz = z_ref[rs, gsl].astype(F32)
            g = yt * (zz * _sigmoid(zz))
            gs = g * lax.rsqrt(jnp.mean(g * g, axis=-1, keepdims=True) + EPS)
            o_ref[rs, gsl] = (gs * nw_ref[:, gsl]).astype(o_ref.dtype)
        else:
            o_ref[rs, gsl] = y


SSD_CPS = 2


def _ssd_call(xs, bc, dtp, sel, *, n_lat, bwd, yf=None, z=None, dsk=None, nw=None):
    cs, w, ecs, cst = dtp
    n_tot = xs.shape[0]
    span = SSD_CPS * SSD_CHUNK
    assert n_lat % span == 0 and n_tot % span == 0
    ncl = n_lat // span
    nch = n_tot // span
    ncc = nch - ncl
    d = 1 if bwd else 0
    if bwd:
        def cid(s):
            return nch - 1 - s
    else:
        def cid(s):
            return jnp.where(s < ncc, ncl + s, s - ncc)
    tile = pl.BlockSpec((span, D_INNER), lambda s: (cid(s), 0))
    full = pl.BlockSpec((span, LANE), lambda s: (cid(s), 0))
    heads = pl.BlockSpec((SSD_HEADS, span), lambda s: (d, cid(s)))
    in_specs = [tile,
                pl.BlockSpec((span, BC_DIM), lambda s: (cid(s), 0)),
                heads, full, full, full,
                pl.BlockSpec((None, SSD_GROUPS, LANE, GROUP_W), lambda s: (d, 0, 0, 0))]
    args = [xs, bc, cst, cs, w, ecs, sel]
    if bwd:
        in_specs += [tile, tile,
                     pl.BlockSpec((1, D_INNER), lambda s: (0, 0)),
                     pl.BlockSpec((1, D_INNER), lambda s: (0, 0))]
        args += [yf, z, dsk, nw]
    return pl.pallas_call(
        functools.partial(_ssd_kernel, bwd=bwd),
        grid=(nch,),
        in_specs=in_specs,
        out_specs=tile,
        out_shape=jax.ShapeDtypeStruct((n_tot, D_INNER), BF16 if bwd else F32),
        scratch_shapes=[pltpu.VMEM((SSD_GROUPS, SSD_STATE, GROUP_W), F32)],
        compiler_params=_cparams(1),
        name="ssd_bwd" if bwd else "ssd_fwd",
    )(*args)


NA_QROWS = 4
NA_KROWS = NA_QROWS + NA_WIN_H
NA_TOK_BLK = 128


def _na_fill_bias(pp_ref, bias_ref):
    lane = lax.broadcasted_iota(jnp.int32, (GRID_W, 2 * GRID_W), 1)
    ninf = jnp.full((GRID_W, 2 * GRID_W), -jnp.inf, F32)
    for v in range(3):
        for kk in range(NA_KROWS):
            for p in range(NA_QROWS // 2):
                i0 = 2 * p
                if v == 1:
                    ok0, ok1 = 0 <= kk - i0 < NA_WIN_H, 0 <= kk - i0 - 1 < NA_WIN_H
                    d0 = kk - i0 + NA_WIN_H // 2 - 1
                else:
                    ok0 = ok1 = kk < NA_WIN_H
                    d0 = kk - i0 + (NA_WIN_H - 1 if v == 0 else NA_WIN_H // 2 - 1)
                if ok0 and ok1:
                    blk = pp_ref[d0]
                elif ok0:
                    blk = jnp.where(lane < GRID_W, pp_ref[d0], ninf)
                elif ok1:
                    blk = jnp.where(lane >= GRID_W, pp_ref[d0], ninf)
                else:
                    blk = ninf
                bias_ref[v, kk * GRID_W:(kk + 1) * GRID_W, p * 2 * GRID_W:(p + 1) * 2 * GRID_W] = blk


def _na_kernel(q_ref, k_ref, v_ref, pp_ref, o_ref, qt_ref, vt_ref, bias_ref, *, n_lat, n_ctx):
    _na_fill_bias(pp_ref, bias_ref)
    rows = n_lat // GRID_W
    qscale = NA_HEAD_DIM ** -0.5 * LOG2E
    nq = NA_QROWS * GRID_W
    nkeys = NA_KROWS * GRID_W
    qblk = nq // NA_TOK_BLK
    kblk = nkeys // NA_TOK_BLK
    lat_blk = n_lat // NA_TOK_BLK
    ctx_blk = n_ctx // NA_TOK_BLK

    def transpose_blocks(p, carry):
        t0 = pl.multiple_of(p * NA_TOK_BLK, NA_TOK_BLK)
        qt_ref[p] = (q_ref[pl.ds(t0, NA_TOK_BLK), :].astype(F32) * qscale).astype(BF16).T
        vt_ref[p] = v_ref[pl.ds(t0, NA_TOK_BLK), :].T
        return carry

    lax.fori_loop(0, lat_blk + ctx_blk, transpose_blocks, 0, unroll=6)

    kc = k_ref[n_lat:n_lat + n_ctx, :]
    vtc = jnp.concatenate([vt_ref[lat_blk + b] for b in range(ctx_blk)], axis=1)

    def softmax_pv(parts):
        m = parts[0][0].max(axis=0, keepdims=True)
        for s, _ in parts[1:]:
            m = jnp.maximum(m, s.max(axis=0, keepdims=True))
        den = 0.0
        acc = 0.0
        for s, vt in parts:
            p = jnp.exp2(s - m)
            den = den + p.sum(axis=0, keepdims=True)
            acc = acc + _dot(vt, p.astype(BF16))
        return acc / den

    def body(t, carry):
        r0 = t * NA_QROWS
        ws = jnp.clip(r0 - NA_WIN_H // 2, 0, rows - NA_WIN_H)
        variant = jnp.where(r0 == 0, 0, jnp.where(r0 == rows - NA_QROWS, 2, 1))
        q0 = pl.multiple_of(r0 * GRID_W, nq)
        k0 = pl.multiple_of(ws * GRID_W, nq)
        kb0 = ws * GRID_W // NA_TOK_BLK
        qt = jnp.concatenate([qt_ref[t * qblk + b] for b in range(qblk)], axis=1)
        vt = jnp.concatenate([vt_ref[kb0 + b] for b in range(kblk)], axis=1)
        s_loc = _dot(k_ref[pl.ds(k0, nkeys), :], qt) + bias_ref[variant]
        s_ctx = _dot(kc, qt)
        ot = softmax_pv([(s_loc, vt), (s_ctx, vtc)])
        o_ref[pl.ds(q0, nq), :] = ot.T.astype(o_ref.dtype)
        return carry

    lax.fori_loop(0, rows // NA_QROWS, body, 0, unroll=8)
    qtc = jnp.concatenate([qt_ref[lat_blk + b] for b in range(ctx_blk)], axis=1)
    otc = softmax_pv([(_dot(kc, qtc), vtc)])
    o_ref[n_lat:n_lat + n_ctx, :] = otc.T.astype(o_ref.dtype)


def _na_call(proj, pairs, *, n_lat):
    n_tot = proj.shape[0]
    n_ctx = n_tot - n_lat
    nblk = n_tot // NA_TOK_BLK
    assert n_ctx >= NA_QROWS * GRID_W
    q0 = PROJ_QKVG // NA_HEAD_DIM

    def col(base):
        return pl.BlockSpec((n_tot, NA_HEAD_DIM), lambda h: (0, base + h))
    return pl.pallas_call(
        functools.partial(_na_kernel, n_lat=n_lat, n_ctx=n_ctx),
        grid=(NA_HEADS,),
        in_specs=[col(q0), col(q0 + NA_HEADS), col(q0 + 2 * NA_HEADS),
                  pl.BlockSpec((None, 2 * NA_WIN_H - 1, GRID_W, 2 * GRID_W), lambda h: (h, 0, 0, 0))],
        out_specs=col(0),
        out_shape=jax.ShapeDtypeStruct((n_tot, D_MODEL), BF16),
        scratch_shapes=[pltpu.VMEM((nblk, NA_HEAD_DIM, NA_TOK_BLK), BF16),
                        pltpu.VMEM((nblk, NA_HEAD_DIM, NA_TOK_BLK), BF16),
                        pltpu.VMEM((3, NA_KROWS * GRID_W, NA_QROWS * GRID_W), F32)],
        compiler_params=_cparams(1),
        name="na_attn",
    )(proj, proj, proj, pairs)


def _na_bias_pairs(rpb):
    c = np.arange(GRID_W)[:, None]
    kc = np.arange(GRID_W)[None, :]
    cstart = np.clip(c - NA_WIN_W // 2, 0, GRID_W - NA_WIN_W)
    col_ok = (kc >= cstart) & (kc < cstart + NA_WIN_W)
    dcol = np.clip(kc - c, -(NA_WIN_W - 1), NA_WIN_W - 1) + NA_WIN_W - 1
    onehot = (dcol.T[None] == np.arange(2 * NA_WIN_W - 1)[:, None, None]).astype(np.float32)
    a = jnp.einsum("hdj,jkc->hdkc", rpb, jnp.asarray(onehot), precision=lax.Precision.HIGHEST)
    a = jnp.where(col_ok.T[None, None], a * LOG2E, -jnp.inf).astype(F32)
    prev = jnp.concatenate([jnp.full_like(a[:, :1], -jnp.inf), a[:, :-1]], axis=1)
    return jnp.concatenate([a, prev], axis=-1)


def _gate_kernel(gn_ref, o_ref_in, ga_ref, gb_ref, w1_ref, w2_ref, out_ref):
    y1 = _dot(gn_ref[...], w1_ref[...].astype(BF16))
    y2 = _dot(o_ref_in[...], w2_ref[...].astype(BF16))
    out = _sigmoid(ga_ref[...].astype(F32)) * y1 + _sigmoid(gb_ref[...].astype(F32)) * y2
    out_ref[...] = out.astype(out_ref.dtype)


def _gate_call(gn, o_na, proj, w_ssd_o, w_na_o, layer, *, rows):
    tm = _pick_tile(rows, 1408)
    tn = 256
    ga0 = (PROJ_QKVG + 3 * D_MODEL) // tn
    gb0 = (PROJ_QKVG + 4 * D_MODEL) // tn
    return pl.pallas_call(
        _gate_kernel,
        grid=(rows // tm, D_MODEL // tn),
        in_specs=[pl.BlockSpec((tm, D_INNER), lambda i, j: (i, 0)),
                  pl.BlockSpec((tm, D_MODEL), lambda i, j: (i, 0)),
                  pl.BlockSpec((tm, tn), lambda i, j: (i, ga0 + j)),
                  pl.BlockSpec((tm, tn), lambda i, j: (i, gb0 + j)),
                  pl.BlockSpec((None, D_INNER, tn), lambda i, j: (layer, 0, j)),
                  pl.BlockSpec((None, D_MODEL, tn), lambda i, j: (layer, 0, j))],
        out_specs=pl.BlockSpec((tm, tn), lambda i, j: (i, j)),
        out_shape=jax.ShapeDtypeStruct((rows, D_MODEL), BF16),
        compiler_params=_cparams(2),
        name="gate_merge",
    )(gn, o_na, proj, proj, w_ssd_o, w_na_o)


def _rope_tables(n_lat, n_ctx):
    t = jnp.arange(n_lat, dtype=jnp.int32)
    n_ax = SSD_STATE // 2
    inv = ROPE_BASE ** (-jnp.arange(0, n_ax, 2, dtype=F32) / n_ax)
    ang_r = (t // GRID_W).astype(F32)[:, None] * inv
    ang_c = (t % GRID_W).astype(F32)[:, None] * inv
    cos_l = jnp.concatenate([jnp.cos(ang_r)] * 2 + [jnp.cos(ang_c)] * 2, axis=1)
    sin_l = jnp.concatenate([-jnp.sin(ang_r), jnp.sin(ang_r), -jnp.sin(ang_c), jnp.sin(ang_c)], axis=1)
    cos_t = jnp.concatenate([cos_l, jnp.ones((n_ctx, LANE), F32)], axis=0)
    sin_t = jnp.concatenate([sin_l, jnp.zeros((n_ctx, LANE), F32)], axis=0)
    return cos_t, sin_t


def _head_selectors():
    sel = np.zeros((2, SSD_GROUPS, LANE, GROUP_W), np.float32)
    for d in range(2):
        for g in range(SSD_GROUPS):
            for e in range(SSD_HPG):
                hd = d * SSD_HEADS + g * SSD_HPG + e
                sel[d, g, hd, e * SSD_HEADDIM:(e + 1) * SSD_HEADDIM] = 1.0
    return jnp.asarray(sel, BF16)


def kernel(x, c, ctx, c_ctx, w_ada, b_ada, g_pre_mix, g_post_mix, g_pre_mlp, g_post_mlp, w_in,
           conv_w, conv_b, a_log, dt_bias, d_skip, ssd_norm, w_ssd_o, rpb, w_na_o, w_out,
           w_mlp1, w_mlp2):
    assert x.shape[0] == 1 and x.shape[2] == D_MODEL
    depth = w_ada.shape[0]
    n_lat = x.shape[1]
    n_ctx = ctx.shape[1]
    n_tot = n_lat + n_ctx
    assert n_lat % (GRID_W * NA_WIN_H) == 0 and n_ctx % 256 == 0 and n_lat % 256 == 0

    cc = jnp.zeros((SUBLANE, D_MODEL), F32).at[0].set(c[0]).at[1].set(c_ctx)
    mod = _ada_call(cc, w_ada, b_ada)
    cos_t, sin_t = _rope_tables(n_lat, n_ctx)
    sel = _head_selectors()
    shift = _conv_shift_matrix()
    w_out_b = w_out.astype(BF16)
    w_mlp2_b = w_mlp2.astype(BF16)
    dsk_all = jnp.repeat(d_skip[:, 0] + d_skip[:, 1], SSD_HEADDIM, axis=-1)

    def row(v):
        return v.reshape(1, -1)

    stream, u = _stream_call(x.reshape(n_lat, D_MODEL), ctx.reshape(n_ctx, D_MODEL),
                             row(g_pre_mix[0]), mod)
    for l in range(depth):
        last = l == depth - 1
        rows_out = n_lat if last else n_tot
        proj, dt_raw = _inproj_call(u, w_in, l)

        xs = _conv_call(proj, shift, conv_w, conv_b, l, 0, D_INNER, None, None, n_lat=n_lat,
                        rope=False, name="conv_x")
        bc = _conv_call(proj, shift, conv_w, conv_b, l, D_INNER, BC_DIM, cos_t, sin_t, n_lat=n_lat,
                        rope=True, name="conv_bc")
        dtp = _dt_call(dt_raw, dt_bias, a_log, l)
        y_f = _ssd_call(xs, bc, dtp, sel, n_lat=n_lat, bwd=False)
        gn = _ssd_call(xs, bc, dtp, sel, n_lat=n_lat, bwd=True, yf=y_f, z=proj,
                       dsk=row(dsk_all[l]), nw=row(ssd_norm[l]))

        o_na = _na_call(proj, _na_bias_pairs(rpb[l]), n_lat=n_lat)
        merged = _gate_call(gn, o_na, proj, w_ssd_o, w_na_o, l, rows=rows_out)
        stream_mid, h = _mm_res_call(merged, w_out_b, stream, row(g_post_mix[l]), mod, l, 2,
                                     D_MODEL, rows=rows_out, n_lat=n_lat, name="out_proj",
                                     nxt=(row(g_pre_mlp[l]), l, 3, 4))
        hidden = _mlp_up_call(h, w_mlp1, l, rows=rows_out)
        nxt = None if last else (row(g_pre_mix[l + 1]), l + 1, 0, 1)
        res = _mm_res_call(hidden, w_mlp2_b, stream_mid, row(g_post_mlp[l]), mod, l, 5, 256,
                           rows=rows_out, n_lat=n_lat, name="mlp_down", nxt=nxt)
        stream, u = (res, None) if last else res
    return stream[None]
```
